```python
import math
import jax, jax.numpy as jnp
from jax import lax
import numpy as np

D_MODEL = 2048
BATCH = 8
SEQ = 2048
DEPTH = 1
DEC_BATCH = 32
DEC_SEQ = 8
PAST_LEN = 16384
PAGE_SIZE = 128

N_META = 16
A_HEADS = 8
A_DK = 128
A_DV = 128
A_WIDTH = A_HEADS * A_DK
A_CHUNK = 64
B_HEADS = 8
B_DH = 64
B_DV = 2 * B_DH
B_WIDTH = B_HEADS * B_DV
Q_BLOCK = 128
ATTN_SCALE = B_DH ** -0.5
N_BUCKETS = 32
MAX_DISTANCE = 128
PEER_HEADS = 8
PEER_NKEYS = 128
PEER_N = PEER_NKEYS * PEER_NKEYS
PEER_DKEY = 256
PEER_TOPK = 16
PEER_BLOCK = 64
IN_SIZES = (A_WIDTH, A_WIDTH, A_HEADS * A_DV, A_HEADS * A_DV, 2 * B_HEADS * B_DH, 2 * B_HEADS * B_DH, B_WIDTH, D_MODEL, D_MODEL)
IN_TOTAL = sum(IN_SIZES)
EPS = 1e-6
NEG_INF = -1e30
F32 = jnp.float32

kernel_name = 'hybrid_hgrn2_diffattn_peer_step'


def _rms(x, g):
    xf = x.astype(F32)
    y = xf * lax.rsqrt(jnp.mean(xf * xf, axis=-1, keepdims=True) + EPS)
    return (y * g.astype(F32)).astype(x.dtype)


def _rel_bucket(dist):
    n = jnp.maximum(dist, 0)
    max_exact = N_BUCKETS // 2
    nf = jnp.maximum(n, 1).astype(F32)
    large = max_exact + (jnp.log(nf / max_exact) / math.log(MAX_DISTANCE / max_exact) * (N_BUCKETS - max_exact)).astype(jnp.int32)
    large = jnp.minimum(large, N_BUCKETS - 1)
    return jnp.where(n < max_exact, n, large)


def _rel_bias(rel_bias, qpos, kpos):
    b = rel_bias[_rel_bucket(qpos[:, None] - kpos[None, :])]
    return jnp.transpose(b, (2, 0, 1)).astype(F32)


def _in_proj(h, w):
    z = jnp.einsum('btd,dn->btn', h, w)
    offs = []
    o = 0
    for s in IN_SIZES[:-1]:
        o += s
        offs.append(o)
    return jnp.split(z, offs, axis=-1)


def _hgrn_inputs(zq, zf, zi, lb):
    nb, t = zq.shape[:2]
    q = zq.astype(F32).reshape(nb, t, A_HEADS, A_DK)
    f = lb + (1.0 - lb) * jax.nn.sigmoid(zf.astype(F32))
    f = f.reshape(nb, t, A_HEADS, A_DK)
    v = zi.astype(F32).reshape(nb, t, A_HEADS, A_DV)
    return q, jnp.log(f), 1.0 - f, v


def _hgrn_chunk(S0, q, logf, k, v):
    C = q.shape[1]
    b = jnp.cumsum(logf, axis=1)
    causal = jnp.tril(jnp.ones((C, C), bool))
    diff = b[:, :, None] - b[:, None, :]
    decay = jnp.exp(jnp.where(causal[None, :, :, None, None], diff, -jnp.inf))
    attn = jnp.einsum('bthc,bshc,btshc->bhts', q, k, decay)
    o = jnp.einsum('bhts,bshv->bthv', attn, v) + jnp.einsum('bthc,bhcv->bthv', q * jnp.exp(b), S0)
    bC = b[:, -1]
    S = jnp.exp(bC)[..., None] * S0 + jnp.einsum('bshc,bshv->bhcv', k * jnp.exp(bC[:, None] - b), v)
    return S, o


def _hgrn_prompt(q, logf, k, v):
    nb = q.shape[0]
    S0 = jnp.zeros((nb, A_HEADS, A_DK, A_DV), F32)
    S, o_meta = _hgrn_chunk(S0, q[:, :N_META], logf[:, :N_META], k[:, :N_META], v[:, :N_META])

    def to_chunks(a):
        r = a[:, N_META:]
        n = r.shape[1] // A_CHUNK
        return jnp.swapaxes(r.reshape(nb, n, A_CHUNK, r.shape[2], r.shape[3]), 0, 1)

    def step(S_c, xs):
        return _hgrn_chunk(S_c, xs[0], xs[1], xs[2], xs[3])

    S, o = lax.scan(step, S, (to_chunks(q), to_chunks(logf), to_chunks(k), to_chunks(v)))
    o = jnp.swapaxes(o, 0, 1).reshape(nb, -1, A_HEADS, A_DV)
    return S, jnp.concatenate([o_meta, o], axis=1)


def _hgrn_out(o, zg, g_norm):
    nb, t = zg.shape[:2]
    y = _rms(o, g_norm) * jax.nn.silu(zg.astype(F32).reshape(nb, t, A_HEADS, A_DV))
    return y.reshape(nb, t, A_HEADS * A_DV).astype(zg.dtype)


def _diff_lambda(lam_p, layer):
    lam_init = 0.8 - 0.6 * math.exp(-0.3 * layer)
    lp = lam_p.astype(F32)
    lam = jnp.exp(jnp.sum(lp[0] * lp[1])) - jnp.exp(jnp.sum(lp[2] * lp[3])) + lam_init
    return lam, lam_init


def _diff_qkv(zq, zk, zv, qg, kg):
    nb, t = zq.shape[:2]
    q = _rms(zq.reshape(nb, t, B_HEADS, 2, B_DH), qg)
    k = _rms(zk.reshape(nb, t, B_HEADS, 2, B_DH), kg)
    v = zv.reshape(nb, t, B_HEADS, B_DV)
    return q, k, v


def _diff_scores(q, k, bias):
    s = jnp.einsum('bqhmd,bkhmd->bmhqk', q, k).astype(F32) * ATTN_SCALE
    return s + bias[None, None]


def _diff_attn_prompt(q, k, v, lam, rel_bias):
    nb, L = q.shape[:2]
    n_qb = -(-L // Q_BLOCK)
    qp = jnp.pad(q, ((0, 0), (0, n_qb * Q_BLOCK - L), (0, 0), (0, 0), (0, 0)))
    kpos = jnp.arange(L, dtype=jnp.int32)
    vf = v.astype(F32)

    def block(i):
        qb = lax.dynamic_slice_in_dim(qp, i * Q_BLOCK, Q_BLOCK, axis=1)
        qpos = i * Q_BLOCK + jnp.arange(Q_BLOCK, dtype=jnp.int32)
        s = _diff_scores(qb, k, _rel_bias(rel_bias, qpos, kpos))
        s = jnp.where(qpos[:, None] >= kpos[None, :], s, NEG_INF)
        p = jax.nn.softmax(s, axis=-1)
        w = p[:, 0] - lam * p[:, 1]
        return jnp.einsum('bhqk,bkhd->bqhd', w, vf)

    o = lax.map(block, jnp.arange(n_qb, dtype=jnp.int32))
    o = jnp.transpose(o, (1, 0, 2, 3, 4)).reshape(nb, n_qb * Q_BLOCK, B_HEADS, B_DV)
    return o[:, :L]


def _online(carry, s, v):
    m, l, acc = carry
    m_new = jnp.maximum(m, jnp.max(s, axis=-1))
    a = jnp.exp(m - m_new)
    p = jnp.exp(s - m_new[..., None])
    l = l * a + jnp.sum(p, axis=-1)
    acc = acc * a[..., None] + jnp.einsum('bmhqk,bkhd->bmhqd', p, v.astype(F32))
    return m_new, l, acc


def _diff_attn_sample(q, k_new, v_new, cache_k, cache_v, page_table, layer, lam, rel_bias):
    nb, t = q.shape[:2]
    n_pages = page_table.shape[1]
    qpos = n_pages * PAGE_SIZE + jnp.arange(t, dtype=jnp.int32)
    page_pos = jnp.arange(PAGE_SIZE, dtype=jnp.int32)
    init = (jnp.full((nb, 2, B_HEADS, t), NEG_INF, F32),
            jnp.zeros((nb, 2, B_HEADS, t), F32),
            jnp.zeros((nb, 2, B_HEADS, t, B_DV), F32))

    def step(carry, p):
        phys = page_table[:, p]
        kb = cache_k[layer, phys].reshape(nb, PAGE_SIZE, B_HEADS, 2, B_DH).astype(q.dtype)
        vb = cache_v[layer, phys]
        s = _diff_scores(q, kb, _rel_bias(rel_bias, qpos, p * PAGE_SIZE + page_pos))
        return _online(carry, s, vb), None

    carry, _ = lax.scan(step, init, jnp.arange(n_pages, dtype=jnp.int32))
    s = _diff_scores(q, k_new, _rel_bias(rel_bias, qpos, qpos))
    s = jnp.where(qpos[:, None] >= qpos[None, :], s, NEG_INF)
    _, l, acc = _online(carry, s, v_new)
    o = acc / l[..., None]
    o = o[:, 0] - lam * o[:, 1]
    return jnp.transpose(o, (0, 2, 1, 3))


def _diff_out(o, g, lam_init, dtype):
    y = _rms(o.astype(F32), g) * (1.0 - lam_init)
    return y.reshape(o.shape[0], o.shape[1], B_WIDTH).astype(dtype)


def _peer(h, wq, k1, k2, u, v):
    nb, t, d = h.shape
    n = nb * t
    hf = h.reshape(n, d)
    q = (hf @ wq).reshape(n, PEER_HEADS, 2, PEER_DKEY // 2)
    s1 = jnp.einsum('nhd,hkd->nhk', q[:, :, 0], k1).astype(F32)
    s2 = jnp.einsum('nhd,hkd->nhk', q[:, :, 1], k2).astype(F32)
    v1, i1 = lax.top_k(s1, PEER_TOPK)
    v2, i2 = lax.top_k(s2, PEER_TOPK)
    cand = (v1[..., :, None] + v2[..., None, :]).reshape(n, PEER_HEADS, PEER_TOPK * PEER_TOPK)
    cidx = (i1[..., :, None] * PEER_NKEYS + i2[..., None, :]).reshape(n, PEER_HEADS, PEER_TOPK * PEER_TOPK)
    sc, pos = lax.top_k(cand, PEER_TOPK)
    eidx = jnp.take_along_axis(cidx, pos, axis=-1)
    g = jax.nn.softmax(sc, axis=-1)
    n_blk = -(-n // PEER_BLOCK)
    pad = n_blk * PEER_BLOCK - n
    hp = jnp.pad(hf, ((0, pad), (0, 0))).reshape(n_blk, PEER_BLOCK, d)
    ep = jnp.pad(eidx, ((0, pad), (0, 0), (0, 0))).reshape(n_blk, PEER_BLOCK, PEER_HEADS, PEER_TOPK)
    gp = jnp.pad(g, ((0, pad), (0, 0), (0, 0))).reshape(n_blk, PEER_BLOCK, PEER_HEADS, PEER_TOPK)

    def blk(args):
        hb, eb, gb = args
        ub = u[eb]
        a = jax.nn.gelu(jnp.einsum('nd,nhkd->nhk', hb, ub).astype(F32), approximate=False)
        vb = v[eb]
        return jnp.einsum('nhk,nhkd->nd', (gb * a).astype(vb.dtype), vb)

    out = lax.map(blk, (hp, ep, gp))
    return out.reshape(n_blk * PEER_BLOCK, d)[:n].reshape(nb, t, d).astype(h.dtype)


def _merge_ffn(x, a_out, b_out, za, zb, a_proj, b_proj, w_out, norm2_g, peer_wq, peer_k1, peer_k2, peer_u, peer_v):
    ya = jnp.einsum('btn,nd->btd', a_out, a_proj)
    yb = jnp.einsum('btn,nd->btd', b_out, b_proj)
    merged = jax.nn.sigmoid(za) * ya + jax.nn.sigmoid(zb) * yb
    x = x + jnp.einsum('btd,de->bte', merged, w_out)
    return x + _peer(_rms(x, norm2_g), peer_wq, peer_k1, peer_k2, peer_u, peer_v)


def setup_inputs(seed: int = 0) -> dict:
    key = jax.random.key(seed)
    ks = jax.random.split(key, 28)
    n_pages = PAST_LEN // PAGE_SIZE
    n_used = DEC_BATCH * n_pages
    n_phys = n_used + n_used // 4

    def nrm(k, shape, scale=1.0):
        return jax.random.normal(k, shape, F32) * scale

    def gain(k, shape):
        return 1.0 + 0.1 * jax.random.normal(k, shape, F32)

    page_table = jax.random.permutation(ks[5], n_phys)[:n_used].astype(jnp.int32).reshape(DEC_BATCH, n_pages)
    return {
        'x_prompt': nrm(ks[0], (BATCH, SEQ, D_MODEL)),
        'x_sample': nrm(ks[1], (DEC_BATCH, DEC_SEQ, D_MODEL)),
        'cache_k': nrm(ks[2], (DEPTH, n_phys, PAGE_SIZE, B_HEADS, 2 * B_DH)),
        'cache_v': nrm(ks[3], (DEPTH, n_phys, PAGE_SIZE, B_HEADS, B_DV)),
        'state_hgrn': nrm(ks[4], (DEPTH, DEC_BATCH, A_HEADS, A_DK, A_DV), 0.3),
        'page_table': page_table,
        'meta_tokens': nrm(ks[6], (N_META, D_MODEL)),
        'lb_logits': nrm(ks[7], (DEPTH + 1, A_WIDTH)),
        'rel_bias': nrm(ks[8], (N_BUCKETS, B_HEADS), 0.5),
        'norm1_g': gain(ks[9], (DEPTH, D_MODEL)),
        'w_in': nrm(ks[10], (DEPTH, D_MODEL, IN_TOTAL), D_MODEL ** -0.5),
        'a_onorm_g': gain(ks[11], (DEPTH, A_DV)),
        'a_proj': nrm(ks[12], (DEPTH, A_HEADS * A_DV, D_MODEL), (A_HEADS * A_DV) ** -0.5),
        'b_qnorm_g': gain(ks[13], (DEPTH, B_DH)),
        'b_knorm_g': gain(ks[14], (DEPTH, B_DH)),
        'b_lambda': nrm(ks[15], (DEPTH, 4, B_DH), 0.1),
        'b_onorm_g': gain(ks[16], (DEPTH, B_DV)),
        'b_proj': nrm(ks[17], (DEPTH, B_WIDTH, D_MODEL), B_WIDTH ** -0.5),
        'w_out': nrm(ks[18], (DEPTH, D_MODEL, D_MODEL), D_MODEL ** -0.5),
        'norm2_g': gain(ks[19], (DEPTH, D_MODEL)),
        'peer_wq': nrm(ks[20], (DEPTH, D_MODEL, PEER_HEADS * PEER_DKEY), D_MODEL ** -0.5),
        'peer_k1': nrm(ks[21], (DEPTH, PEER_HEADS, PEER_NKEYS, PEER_DKEY // 2), (PEER_DKEY // 2) ** -0.5),
        'peer_k2': nrm(ks[22], (DEPTH, PEER_HEADS, PEER_NKEYS, PEER_DKEY // 2), (PEER_DKEY // 2) ** -0.5),
        'peer_u': nrm(ks[23], (DEPTH, PEER_N, D_MODEL), D_MODEL ** -0.5),
        'peer_v': nrm(ks[24], (DEPTH, PEER_N, D_MODEL), PEER_HEADS ** -0.5),
    }


def reference(x_prompt, x_sample, cache_k, cache_v, state_hgrn, page_table, meta_tokens, lb_logits, rel_bias,
              norm1_g, w_in, a_onorm_g, a_proj, b_qnorm_g, b_knorm_g, b_lambda, b_onorm_g, b_proj, w_out,
              norm2_g, peer_wq, peer_k1, peer_k2, peer_u, peer_v):
    lb_all = jnp.cumsum(jax.nn.softmax(lb_logits.astype(F32), axis=0), axis=0)
    nbp = x_prompt.shape[0]
    meta = jnp.broadcast_to(meta_tokens[None].astype(x_prompt.dtype), (nbp, N_META, D_MODEL))
    xp = jnp.concatenate([meta, x_prompt], axis=1)
    xs = x_sample
    kp_l, vp_l, sp_l, ks_l, vs_l, ss_l = [], [], [], [], [], []
    for l in range(DEPTH):
        lam, lam_init = _diff_lambda(b_lambda[l], l)
        lb = lb_all[l]
        z = _in_proj(_rms(xp, norm1_g[l]), w_in[l])
        qa, lfa, ka, va = _hgrn_inputs(z[0], z[1], z[2], lb)
        S_p, oa = _hgrn_prompt(qa, lfa, ka, va)
        a_out = _hgrn_out(oa, z[3], a_onorm_g[l])
        qb, kb, vb = _diff_qkv(z[4], z[5], z[6], b_qnorm_g[l], b_knorm_g[l])
        ob = _diff_attn_prompt(qb, kb, vb, lam, rel_bias)
        b_out = _diff_out(ob, b_onorm_g[l], lam_init, xp.dtype)
        kp_l.append(kb.reshape(kb.shape[0], kb.shape[1], B_HEADS, 2 * B_DH))
        vp_l.append(vb)
        sp_l.append(S_p.astype(xp.dtype))
        xp = _merge_ffn(xp, a_out, b_out, z[7], z[8], a_proj[l], b_proj[l], w_out[l], norm2_g[l],
                        peer_wq[l], peer_k1[l], peer_k2[l], peer_u[l], peer_v[l])
        z = _in_proj(_rms(xs, norm1_g[l]), w_in[l])
        qa, lfa, ka, va = _hgrn_inputs(z[0], z[1], z[2], lb)
        S_s, oa = _hgrn_chunk(state_hgrn[l].astype(F32), qa, lfa, ka, va)
        a_out = _hgrn_out(oa, z[3], a_onorm_g[l])
        qb, kb, vb = _diff_qkv(z[4], z[5], z[6], b_qnorm_g[l], b_knorm_g[l])
        ob = _diff_attn_sample(qb, kb, vb, cache_k, cache_v, page_table, l, lam, rel_bias)
        b_out = _diff_out(ob, b_onorm_g[l], lam_init, xs.dtype)
        ks_l.append(kb.reshape(kb.shape[0], kb.shape[1], B_HEADS, 2 * B_DH))
        vs_l.append(vb)
        ss_l.append(S_s.astype(state_hgrn.dtype))
        xs = _merge_ffn(xs, a_out, b_out, z[7], z[8], a_proj[l], b_proj[l], w_out[l], norm2_g[l],
                        peer_wq[l], peer_k1[l], peer_k2[l], peer_u[l], peer_v[l])
    y_prompt = xp[:, N_META:]
    y_sample = xs
    return (y_prompt, y_sample, jnp.stack(kp_l), jnp.stack(vp_l), jnp.stack(sp_l), jnp.stack(ks_l), jnp.stack(vs_l), jnp.stack(ss_l))
```

```python
import functools
import math

import jax
import jax.numpy as jnp
from jax import lax
from jax.experimental import pallas as pl
from jax.experimental.pallas import tpu as pltpu

F32 = jnp.float32
BF16 = jnp.bfloat16
I32 = jnp.int32

LANE = 128
D_MODEL = 2048
N_META = 16
PAD_FRONT = LANE - N_META
A_HEADS = 8
A_DK = 128
A_DV = 128
B_HEADS = 8
B_DH = 64
B_DV = 128
N_BUCKETS = 32
MAX_DISTANCE = 128
PEER_HEADS = 8
PEER_NKEYS = 128
PEER_TOPK = 16
EPS = 1e-6
NEG_INF = -1e30
ATTN_SCALE = B_DH ** -0.5
SQRT_HALF = 0.7071067811865476
HGRN_CHUNK = 16
VMEM_LIMIT = 56 * 1024 * 1024

OFF_AQ, OFF_AF, OFF_AI, OFF_AG = 0, 1024, 2048, 3072
OFF_BQ, OFF_BK, OFF_BV = 4096, 5120, 6144
OFF_GA, OFF_GB = 7168, 9216
IN_TOTAL = 11264

_NT = (((1,), (1,)), ((), ()))
_TN = (((0,), (0,)), ((), ()))


def _cparams(sem):
    return pltpu.CompilerParams(dimension_semantics=sem, vmem_limit_bytes=VMEM_LIMIT)


def _row_tile(n, prefs=(768, 512, 384, 256, 128)):
    for t in prefs:
        if n % t == 0:
            return t
    raise ValueError(f"token count {n} is not a multiple of 128")


def _rms_matmul_kernel(x_ref, g_ref, w_ref, o_ref, h_ref):
    @pl.when(pl.program_id(1) == 0)
    def _():
        x = x_ref[...]
        ms = jnp.mean(x * x, axis=-1, keepdims=True)
        h_ref[...] = (x * lax.rsqrt(ms + EPS) * g_ref[...]).astype(BF16)

    o_ref[...] = jnp.dot(h_ref[...], w_ref[...], preferred_element_type=F32)


def _rms_matmul(x, g, w_bf16, tn):
    n, d = x.shape
    nc = w_bf16.shape[1]
    tm = _row_tile(n)
    return pl.pallas_call(
        _rms_matmul_kernel,
        grid=(n // tm, nc // tn),
        in_specs=[
            pl.BlockSpec((tm, d), lambda i, j: (i, 0)),
            pl.BlockSpec((1, d), lambda i, j: (0, 0)),
            pl.BlockSpec((d, tn), lambda i, j: (0, j)),
        ],
        out_specs=pl.BlockSpec((tm, tn), lambda i, j: (i, j)),
        out_shape=jax.ShapeDtypeStruct((n, nc), F32),
        scratch_shapes=[pltpu.VMEM((tm, d), BF16)],
        compiler_params=_cparams(("arbitrary", "arbitrary")),
        name="rms_in_proj",
    )(x, g.reshape(1, d), w_bf16)


def _hgrn_kernel(lbl_ref, gn_ref, q_ref, f_ref, i_ref, g_ref, s0_ref, o_ref, sfin_ref, st_scr, *, c, n_chunks):
    st_scr[...] = s0_ref[0, 0].T
    lg = lbl_ref[...]
    e = jnp.exp(lg - jnp.max(lg, axis=0, keepdims=True))
    lb = e[0:1] / jnp.sum(e, axis=0, keepdims=True)
    gn = gn_ref[...]
    row = lax.broadcasted_iota(I32, (c, A_DK), 0)
    ones = jnp.ones((A_DK, A_DV), BF16)

    def chunk(ci, carry):
        r0 = pl.multiple_of(ci * c, c)
        q = q_ref[pl.ds(r0, c), :]
        zf = f_ref[pl.ds(r0, c), :]
        v = i_ref[pl.ds(r0, c), :]
        zg = g_ref[pl.ds(r0, c), :]
        f = lb + (1.0 - lb) * jax.nn.sigmoid(zf)
        k = 1.0 - f
        b = jnp.log(f)
        sh = 1
        while sh < c:
            b = b + jnp.where(row >= sh, pltpu.roll(b, sh, 0), 0.0)
            sh *= 2
        ps = []
        for s in range(c):
            d = jnp.exp(jnp.where(row >= s, b - b[s:s + 1, :], NEG_INF))
            ps.append(q * d * k[s:s + 1, :])
        p = jnp.concatenate(ps, axis=0).astype(BF16)
        a = jnp.dot(p, ones, preferred_element_type=F32)
        o = a[0:c] * v[0:1, :]
        for s in range(1, c):
            o = o + a[s * c:(s + 1) * c] * v[s:s + 1, :]
        st = st_scr[...]
        qe = (q * jnp.exp(b)).astype(BF16)
        o = o + lax.dot_general(qe, st.astype(BF16), _NT, preferred_element_type=F32)
        bc = b[c - 1:c, :]
        kt = (k * jnp.exp(bc - b)).astype(BF16)
        upd = lax.dot_general(v.astype(BF16), kt, _TN, preferred_element_type=F32)
        st_scr[...] = st * jnp.exp(bc) + upd
        ms = jnp.mean(o * o, axis=-1, keepdims=True)
        o_ref[pl.ds(r0, c), :] = o * lax.rsqrt(ms + EPS) * gn * (zg * jax.nn.sigmoid(zg))
        return carry

    lax.fori_loop(0, n_chunks, chunk, 0)
    sfin_ref[0, 0] = st_scr[...].T


def _hgrn(z, lb_logits, gn, s0, *, row0, nb, t_len, c):
    assert row0 % t_len == 0 and t_len % c == 0
    rb0 = row0 // t_len
    zspec = lambda off: pl.BlockSpec((t_len, A_DK), lambda b, h, off=off: (rb0 + b, off // A_DK + h))
    return pl.pallas_call(
        functools.partial(_hgrn_kernel, c=c, n_chunks=t_len // c),
        grid=(nb, A_HEADS),
        in_specs=[
            pl.BlockSpec((lb_logits.shape[0], A_DK), lambda b, h: (0, h)),
            pl.BlockSpec((1, A_DV), lambda b, h: (0, 0)),
            zspec(OFF_AQ), zspec(OFF_AF), zspec(OFF_AI), zspec(OFF_AG),
            pl.BlockSpec((1, 1, A_DK, A_DV), lambda b, h: (b, h, 0, 0)),
        ],
        out_specs=[
            pl.BlockSpec((t_len, A_DV), lambda b, h: (b, h)),
            pl.BlockSpec((1, 1, A_DK, A_DV), lambda b, h: (b, h, 0, 0)),
        ],
        out_shape=[
            jax.ShapeDtypeStruct((nb * t_len, A_HEADS * A_DV), F32),
            jax.ShapeDtypeStruct((nb, A_HEADS, A_DK, A_DV), F32),
        ],
        scratch_shapes=[pltpu.VMEM((A_DV, A_DK), F32)],
        compiler_params=_cparams(("arbitrary", "arbitrary")),
        name=f"hgrn_c{c}",
    )(lb_logits, gn.reshape(1, A_DV), z, z, z, z, s0)


def _qk_norm_kernel(zq_ref, zk_ref, zv_ref, qg_ref, kg_ref, qn_ref, kn_ref, knb_ref, vb_ref):
    r = lax.broadcasted_iota(I32, (LANE, LANE), 0) // B_DH
    cc = lax.broadcasted_iota(I32, (LANE, LANE), 1) // B_DH
    seg = jnp.where(r == cc, 1.0, 0.0).astype(BF16)

    def norm(z, g):
        sq = z * z
        hi = sq.astype(BF16)
        lo = (sq - hi.astype(F32)).astype(BF16)
        ss = jnp.dot(hi, seg, preferred_element_type=F32) + jnp.dot(lo, seg, preferred_element_type=F32)
        return z * lax.rsqrt(ss * (1.0 / B_DH) + EPS) * g

    qg = qg_ref[...]
    kg = kg_ref[...]
    for h in range(B_HEADS):
        sl = slice(h * LANE, (h + 1) * LANE)
        qn_ref[:, sl] = (norm(zq_ref[:, sl], qg) * ATTN_SCALE).astype(BF16)
        kn = norm(zk_ref[:, sl], kg)
        kn_ref[:, sl] = kn
        knb_ref[:, sl] = kn.astype(BF16)
    vb_ref[...] = zv_ref[...].astype(BF16)


def _qk_norm(z, qg, kg):
    n = z.shape[0]
    tm = _row_tile(n)
    w = B_HEADS * 2 * B_DH
    zspec = lambda off: pl.BlockSpec((tm, w), lambda i, off=off: (i, off // w))
    ospec = pl.BlockSpec((tm, w), lambda i: (i, 0))
    gspec = pl.BlockSpec((1, LANE), lambda i: (0, 0))
    return pl.pallas_call(
        _qk_norm_kernel,
        grid=(n // tm,),
        in_specs=[zspec(OFF_BQ), zspec(OFF_BK), zspec(OFF_BV), gspec, gspec],
        out_specs=[ospec, ospec, ospec, ospec],
        out_shape=[
            jax.ShapeDtypeStruct((n, w), BF16),
            jax.ShapeDtypeStruct((n, w), F32),
            jax.ShapeDtypeStruct((n, w), BF16),
            jax.ShapeDtypeStruct((n, w), BF16),
        ],
        compiler_params=_cparams(("arbitrary",)),
        name="qk_norm",
    )(z, z, z, jnp.tile(qg, 2).reshape(1, LANE), jnp.tile(kg, 2).reshape(1, LANE))


def _rel_bucket(dist):
    n = jnp.maximum(dist, 0)
    max_exact = N_BUCKETS // 2
    nf = jnp.maximum(n, 1).astype(F32)
    large = max_exact + (jnp.log(nf / max_exact) / math.log(MAX_DISTANCE / max_exact)
                         * (N_BUCKETS - max_exact)).astype(I32)
    large = jnp.minimum(large, N_BUCKETS - 1)
    return jnp.where(n < max_exact, n, large)


def _bias_from_bucket(bucket, rb_ref, h):
    val = jnp.zeros(bucket.shape, F32)
    for bk in range(N_BUCKETS):
        val = jnp.where(bucket == bk, rb_ref[bk, h], val)
    return val


def _diff_lambda(lam_ref):
    lp = lam_ref[...]
    lam_init = 0.8 - 0.6 * math.exp(-0.3 * 0)
    lam = (jnp.exp(jnp.sum(lp[0:1] * lp[1:2], axis=-1, keepdims=True))
           - jnp.exp(jnp.sum(lp[2:3] * lp[3:4], axis=-1, keepdims=True)) + lam_init)
    return lam, lam_init


def _split_maps(qh):
    lane = lax.broadcasted_iota(I32, qh.shape, 1)
    zero = jnp.zeros_like(qh)
    return jnp.concatenate([jnp.where(lane < B_DH, qh, zero), jnp.where(lane >= B_DH, qh, zero)], axis=0)


def _online_update(carry, s, vj):
    m, l, acc = carry
    m_new = jnp.maximum(m, jnp.max(s, axis=-1, keepdims=True))
    a = jnp.exp(m - m_new)
    p = jnp.exp(s - m_new)
    l = l * a + jnp.sum(p, axis=-1, keepdims=True)
    acc = acc * a + jnp.dot(p.astype(BF16), vj, preferred_element_type=F32)
    return m_new, l, acc


def _diff_finish(carry, t, lam, lam_init, og):
    _, l, acc = carry
    o = acc[0:t] / l[0:t] - lam * (acc[t:2 * t] / l[t:2 * t])
    ms = jnp.mean(o * o, axis=-1, keepdims=True)
    return o * lax.rsqrt(ms + EPS) * og * (1.0 - lam_init)


def _attn_prompt_kernel(rb_ref, lam_ref, og_ref, q_ref, k_ref, v_ref, o_ref, bias_scr):
    i = pl.program_id(1)
    tq = q_ref.shape[0]

    @pl.when((pl.program_id(0) == 0) & (i == 0))
    def _():
        r = lax.broadcasted_iota(I32, (tq, tq), 0)
        cc = lax.broadcasted_iota(I32, (tq, tq), 1)
        for blk in range(2):
            d = r - cc + blk * tq
            bucket = _rel_bucket(d)
            for h in range(B_HEADS):
                val = _bias_from_bucket(bucket, rb_ref, h)
                if blk == 0:
                    val = jnp.where(d >= 0, val, NEG_INF)
                bias_scr[blk, h, 0:tq] = val
                bias_scr[blk, h, tq:2 * tq] = val

    lam, lam_init = _diff_lambda(lam_ref)
    og = og_ref[...]
    lane = lax.broadcasted_iota(I32, (1, tq), 1)

    for h in range(B_HEADS):
        sl = slice(h * LANE, (h + 1) * LANE)
        q2 = _split_maps(q_ref[:, sl])
        far = rb_ref[N_BUCKETS - 1, h]

        def scores(j):
            k0 = pl.multiple_of(j * tq, tq)
            s = lax.dot_general(q2, k_ref[pl.ds(k0, tq), sl], _NT, preferred_element_type=F32)
            pad = jnp.where(k0 + lane < PAD_FRONT, NEG_INF, 0.0)
            return s, pad, v_ref[pl.ds(k0, tq), sl]

        def far_step(j, carry):
            s, pad, vj = scores(j)
            return _online_update(carry, s + (pad + far), vj)

        def near_step(carry):
            s, pad, vj = scores(i - 1)
            return _online_update(carry, s + bias_scr[1, h] + pad, vj)

        carry = (jnp.full((2 * tq, 1), NEG_INF, F32), jnp.zeros((2 * tq, 1), F32),
                 jnp.zeros((2 * tq, B_DV), F32))
        carry = lax.fori_loop(0, jnp.maximum(i - 1, 0), far_step, carry)
        carry = lax.cond(i >= 1, near_step, lambda cr: cr, carry)
        s, pad, vj = scores(i)
        carry = _online_update(carry, s + bias_scr[0, h] + pad, vj)
        o_ref[:, sl] = _diff_finish(carry, tq, lam, lam_init, og)


def _attn_prompt(qn, knb, vb, rel_bias, b_lambda, og, *, nb, lp):
    tq = LANE
    nq = lp // tq
    w = B_HEADS * B_DV
    return pl.pallas_call(
        _attn_prompt_kernel,
        grid=(nb, nq),
        in_specs=[
            pl.BlockSpec(memory_space=pltpu.SMEM),
            pl.BlockSpec((4, B_DH), lambda b, i: (0, 0)),
            pl.BlockSpec((1, B_DV), lambda b, i: (0, 0)),
            pl.BlockSpec((tq, w), lambda b, i: (b * nq + i, 0)),
            pl.BlockSpec((lp, w), lambda b, i: (b, 0)),
            pl.BlockSpec((lp, w), lambda b, i: (b, 0)),
        ],
        out_specs=pl.BlockSpec((tq, w), lambda b, i: (b * nq + i, 0)),
        out_shape=jax.ShapeDtypeStruct((nb * lp, w), F32),
        scratch_shapes=[pltpu.VMEM((2, B_HEADS, 2 * tq, tq), F32)],
        compiler_params=_cparams(("arbitrary", "arbitrary")),
        name="attn_prompt",
    )(rel_bias, b_lambda, og.reshape(1, B_DV), qn, knb, vb)


def _attn_decode_kernel(pt_ref, rb_ref, lam_ref, og_ref, q_ref, kn_ref, vn_ref, *rest, pages_per_step, n_pages):
    kp = rest[:pages_per_step]
    vp = rest[pages_per_step:2 * pages_per_step]
    o_ref, bias_scr, m_scr, l_scr, acc_scr = rest[2 * pages_per_step:]
    del pt_ref
    b = pl.program_id(0)
    p = pl.program_id(1)
    n_steps = pl.num_programs(1)
    t = q_ref.shape[0]
    page = kp[0].shape[2]
    rows = 2 * t

    @pl.when((b == 0) & (p == 0))
    def _():
        r = lax.broadcasted_iota(I32, (rows, page), 0) % t
        cc = lax.broadcasted_iota(I32, (rows, page), 1)
        d_last = page - cc + r
        d_self = r - cc
        for h in range(B_HEADS):
            bias_scr[0, h] = _bias_from_bucket(_rel_bucket(d_last), rb_ref, h)
            bias_scr[1, h] = jnp.where((d_self >= 0) & (cc < t),
                                       _bias_from_bucket(_rel_bucket(d_self), rb_ref, h), NEG_INF)

    @pl.when(p == 0)
    def _():
        m_scr[...] = jnp.full(m_scr.shape, NEG_INF, F32)
        l_scr[...] = jnp.zeros(l_scr.shape, F32)
        acc_scr[...] = jnp.zeros(acc_scr.shape, F32)

    for h in range(B_HEADS):
        sl = slice(h * LANE, (h + 1) * LANE)
        q2 = _split_maps(q_ref[:, sl].astype(BF16))
        far = rb_ref[N_BUCKETS - 1, h]
        carry = (m_scr[h], l_scr[h], acc_scr[h])
        for i in range(pages_per_step):
            s = lax.dot_general(q2, kp[i][0, 0, :, h, :].astype(BF16), _NT, preferred_element_type=F32)
            is_last = p * pages_per_step + i == n_pages - 1
            s = s + jnp.where(is_last, bias_scr[0, h], far)
            carry = _online_update(carry, s, vp[i][0, 0, :, h, :].astype(BF16))
        m_scr[h], l_scr[h], acc_scr[h] = carry

    @pl.when(p == n_steps - 1)
    def _():
        lam, lam_init = _diff_lambda(lam_ref)
        og = og_ref[...]
        zpad = jnp.zeros((page - t, LANE), BF16)
        for h in range(B_HEADS):
            sl = slice(h * LANE, (h + 1) * LANE)
            q2 = _split_maps(q_ref[:, sl].astype(BF16))
            kn = jnp.concatenate([kn_ref[:, sl].astype(BF16), zpad], axis=0)
            vn = jnp.concatenate([vn_ref[:, sl].astype(BF16), zpad], axis=0)
            s = lax.dot_general(q2, kn, _NT, preferred_element_type=F32) + bias_scr[1, h]
            carry = _online_update((m_scr[h], l_scr[h], acc_scr[h]), s, vn)
            o_ref[:, sl] = _diff_finish(carry, t, lam, lam_init, og)


def _attn_decode(qn, kn, vn, cache_k, cache_v, page_table, rel_bias, b_lambda, og, *, nb, t):
    n_pages = page_table.shape[1]
    n_phys, page = cache_k.shape[1], cache_k.shape[2]
    w = B_HEADS * B_DV
    pps = 8
    while n_pages % pps:
        pps //= 2
    tok = pl.BlockSpec((t, w), lambda b, p, pt: (b, 0))
    pspec = lambda i: pl.BlockSpec((1, 1, page, B_HEADS, B_DV),
                                   lambda b, p, pt, i=i: (0, pt[b, p * pps + i], 0, 0, 0))
    grid_spec = pltpu.PrefetchScalarGridSpec(
        num_scalar_prefetch=1,
        grid=(nb, n_pages // pps),
        in_specs=[
            pl.BlockSpec(memory_space=pltpu.SMEM),
            pl.BlockSpec((4, B_DH), lambda b, p, pt: (0, 0)),
            pl.BlockSpec((1, B_DV), lambda b, p, pt: (0, 0)),
            tok, tok, tok,
        ] + [pspec(i) for i in range(pps)] + [pspec(i) for i in range(pps)],
        out_specs=pl.BlockSpec((t, w), lambda b, p, pt: (b, 0)),
        scratch_shapes=[
            pltpu.VMEM((2, B_HEADS, 2 * t, page), F32),
            pltpu.VMEM((B_HEADS, 2 * t, 1), F32),
            pltpu.VMEM((B_HEADS, 2 * t, 1), F32),
            pltpu.VMEM((B_HEADS, 2 * t, B_DV), F32),
        ],
    )
    return pl.pallas_call(
        functools.partial(_attn_decode_kernel, pages_per_step=pps, n_pages=n_pages),
        grid_spec=grid_spec,
        out_shape=jax.ShapeDtypeStruct((nb * t, w), F32),
        compiler_params=_cparams(("arbitrary", "arbitrary")),
        name="attn_decode",
    )(page_table, rel_bias, b_lambda, og.reshape(1, B_DV), qn, kn, vn, *([cache_k] * pps), *([cache_v] * pps))


def _merge_kernel(a_ref, b_ref, za_ref, zb_ref, wa_ref, wb_ref, o_ref):
    ya = jnp.dot(a_ref[...].astype(BF16), wa_ref[...], preferred_element_type=F32)
    yb = jnp.dot(b_ref[...].astype(BF16), wb_ref[...], preferred_element_type=F32)
    o_ref[...] = (jax.nn.sigmoid(za_ref[...]) * ya + jax.nn.sigmoid(zb_ref[...]) * yb).astype(BF16)


def _merge(a_out, b_out, z, wa, wb):
    n, wa_in = a_out.shape
    d = wa.shape[1]
    tm, tn = _row_tile(n), 1024
    return pl.pallas_call(
        _merge_kernel,
        grid=(n // tm, d // tn),
        in_specs=[
            pl.BlockSpec((tm, wa_in), lambda i, j: (i, 0)),
            pl.BlockSpec((tm, b_out.shape[1]), lambda i, j: (i, 0)),
            pl.BlockSpec((tm, tn), lambda i, j: (i, OFF_GA // tn + j)),
            pl.BlockSpec((tm, tn), lambda i, j: (i, OFF_GB // tn + j)),
            pl.BlockSpec((wa_in, tn), lambda i, j: (0, j)),
            pl.BlockSpec((wb.shape[0], tn), lambda i, j: (0, j)),
        ],
        out_specs=pl.BlockSpec((tm, tn), lambda i, j: (i, j)),
        out_shape=jax.ShapeDtypeStruct((n, d), BF16),
        compiler_params=_cparams(("arbitrary", "arbitrary")),
        name="gated_merge",
    )(a_out, b_out, z, z, wa, wb)


def _out_proj_kernel(x_ref, m_ref, w_ref, g_ref, x1_ref, h2_ref):
    x1 = x_ref[...] + jnp.dot(m_ref[...], w_ref[...], preferred_element_type=F32)
    x1_ref[...] = x1
    ms = jnp.mean(x1 * x1, axis=-1, keepdims=True)
    h2_ref[...] = (x1 * lax.rsqrt(ms + EPS) * g_ref[...]).astype(BF16)


def _out_proj(x, merged, w_out, g2):
    n, d = x.shape
    tm = _row_tile(n, (256, 128))
    row = pl.BlockSpec((tm, d), lambda i: (i, 0))
    return pl.pallas_call(
        _out_proj_kernel,
        grid=(n // tm,),
        in_specs=[row, row, pl.BlockSpec((d, d), lambda i: (0, 0)), pl.BlockSpec((1, d), lambda i: (0, 0))],
        out_specs=[row, row],
        out_shape=[jax.ShapeDtypeStruct((n, d), F32), jax.ShapeDtypeStruct((n, d), BF16)],
        compiler_params=_cparams(("arbitrary",)),
        name="out_proj_norm2",
    )(x, merged, w_out, g2.reshape(1, d))


def _peer_scores_kernel(h_ref, wq_ref, k1_ref, k2_ref, s_ref):
    q = jnp.dot(h_ref[...], wq_ref[...], preferred_element_type=F32).astype(BF16)
    half = q.shape[1] // (2 * PEER_HEADS)
    for h in range(PEER_HEADS):
        for c, kr in enumerate((k1_ref, k2_ref)):
            qs = q[:, (2 * h + c) * half:(2 * h + c + 1) * half]
            s_ref[2 * h + c] = lax.dot_general(kr[h], qs, _NT, preferred_element_type=F32)


def _peer_scores(h2, wq, k1, k2):
    n, d = h2.shape
    tm = _row_tile(n, (256, 128))
    kspec = pl.BlockSpec(k1.shape, lambda i: (0, 0, 0))
    return pl.pallas_call(
        _peer_scores_kernel,
        grid=(n // tm,),
        in_specs=[pl.BlockSpec((tm, d), lambda i: (i, 0)), pl.BlockSpec(wq.shape, lambda i: (0, 0)), kspec, kspec],
        out_specs=pl.BlockSpec((2 * PEER_HEADS, PEER_NKEYS, tm), lambda i: (0, 0, i)),
        out_shape=jax.ShapeDtypeStruct((2 * PEER_HEADS, PEER_NKEYS, n), F32),
        compiler_params=_cparams(("arbitrary",)),
        name="peer_scores",
    )(h2, wq, k1, k2)


def _cand_layout():
    k = PEER_TOPK
    groups = [(0, k)] + [(j, 8) for j in range(1, 8)]
    pos = []
    for j, rows in groups:
        for l in range(rows):
            pos.append(j * k + l if (j + 1) * (l + 1) <= k else 1 << 20)
    pos += [j * k for j in range(8, k)]
    return groups, pos


def _topk_rows(s, k):
    n_rows = s.shape[0]
    row = lax.broadcasted_iota(I32, s.shape, 0)
    vals, idxs = [], []
    for _ in range(k):
        m = jnp.max(s, axis=0, keepdims=True)
        idx = jnp.min(jnp.where(s == m, row, n_rows), axis=0, keepdims=True)
        vals.append(m)
        idxs.append(idx)
        s = jnp.where(row == idx, NEG_INF, s)
    return jnp.concatenate(vals, axis=0), jnp.concatenate(idxs, axis=0)


def _peer_topk_kernel(s_ref, pos_ref, i1_ref, i2_ref, g_ref):
    k = PEER_TOPK
    groups, _ = _cand_layout()
    pos = pos_ref[...]
    big = 1 << 20
    e_rows, g_rows = [], []
    for h in range(PEER_HEADS):
        v1, i1 = _topk_rows(s_ref[2 * h], k)
        v2, i2 = _topk_rows(s_ref[2 * h + 1], k)
        cand, cidx = [], []
        for j, rows in groups:
            cand.append(v1[j:j + 1] + v2[0:rows])
            cidx.append(i1[j:j + 1] * PEER_NKEYS + i2[0:rows])
        cand.append(v1[8:k] + v2[0:1])
        cidx.append(i1[8:k] * PEER_NKEYS + i2[0:1])
        cand = jnp.where(pos < big, jnp.concatenate(cand, axis=0), NEG_INF)
        cidx = jnp.concatenate(cidx, axis=0)
        sc, ex = [], []
        for _ in range(k):
            m = jnp.max(cand, axis=0, keepdims=True)
            pidx = jnp.min(jnp.where(cand == m, pos, big), axis=0, keepdims=True)
            sel = pos == pidx
            sc.append(m)
            ex.append(jnp.max(jnp.where(sel, cidx, -1), axis=0, keepdims=True))
            cand = jnp.where(sel, NEG_INF, cand)
        sc = jnp.concatenate(sc, axis=0)
        ew = jnp.exp(sc - sc[0:1])
        g_rows.append(ew / jnp.sum(ew, axis=0, keepdims=True))
        e_rows.append(jnp.concatenate(ex, axis=0))
    e_all = jnp.concatenate(e_rows, axis=0)
    g_all = jnp.concatenate(g_rows, axis=0)
    i1_ref[...] = (e_all >> 7).astype(F32).T
    i2_ref[...] = (e_all & (PEER_NKEYS - 1)).astype(F32).T
    g_ref[...] = g_all.T


def _peer_topk(scores_t):
    n = scores_t.shape[2]
    tl = LANE
    _, pos = _cand_layout()
    pos_tab = jnp.broadcast_to(jnp.asarray(pos, I32)[:, None], (len(pos), tl))
    nsel = PEER_HEADS * PEER_TOPK
    ospec = pl.BlockSpec((tl, nsel), lambda i: (i, 0))
    oshape = jax.ShapeDtypeStruct((n, nsel), F32)
    return pl.pallas_call(
        _peer_topk_kernel,
        grid=(n // tl,),
        in_specs=[pl.BlockSpec((2 * PEER_HEADS, PEER_NKEYS, tl), lambda i: (0, 0, i)),
                  pl.BlockSpec(pos_tab.shape, lambda i: (0, 0))],
        out_specs=[ospec, ospec, ospec],
        out_shape=[oshape, oshape, oshape],
        compiler_params=_cparams(("arbitrary",)),
        name="peer_topk",
    )(scores_t, pos_tab)


def _peer_gates_kernel(i1_ref, i2_ref, g_ref, o_ref, x_scr):
    tb = i1_ref.shape[0]
    grp = 16
    sub = lax.broadcasted_iota(I32, (PEER_NKEYS, LANE), 0).astype(F32)

    def group(gi, carry):
        r0 = pl.multiple_of(gi * grp, grp)
        for t in range(grp):
            i1 = i1_ref[pl.ds(r0 + t, 1), :]
            i2 = i2_ref[pl.ds(r0 + t, 1), :]
            g = g_ref[pl.ds(r0 + t, 1), :]
            p1 = jnp.where(i1 == sub, g, 0.0).astype(BF16)
            p2 = jnp.where(i2 == sub, 1.0, 0.0).astype(BF16)
            x_scr[t * PEER_NKEYS:(t + 1) * PEER_NKEYS, :] = lax.dot_general(
                p1, p2, _NT, preferred_element_type=F32)

        def relayout(i1, cr):
            o_ref[i1, pl.ds(r0, grp), :] = x_scr[pl.ds(i1, grp, stride=PEER_NKEYS), :].astype(BF16)
            return cr

        lax.fori_loop(0, PEER_NKEYS, relayout, 0, unroll=8)
        return carry

    lax.fori_loop(0, tb // grp, group, 0)


def _peer_gates(i1, i2, g):
    n, nsel = i1.shape
    tb = LANE
    ispec = pl.BlockSpec((tb, nsel), lambda i: (i, 0))
    return pl.pallas_call(
        _peer_gates_kernel,
        grid=(n // tb,),
        in_specs=[ispec, ispec, ispec],
        out_specs=pl.BlockSpec((PEER_NKEYS, tb, PEER_NKEYS), lambda i: (0, i, 0)),
        out_shape=jax.ShapeDtypeStruct((PEER_NKEYS, n, PEER_NKEYS), BF16),
        scratch_shapes=[pltpu.VMEM((16 * PEER_NKEYS, PEER_NKEYS), F32)],
        compiler_params=_cparams(("arbitrary",)),
        name="peer_gates",
    )(i1, i2, g)


def _peer_dense_kernel(h_ref, x1_ref, u_ref, v_ref, g_ref, o_ref):
    @pl.when(pl.program_id(1) == 0)
    def _():
        o_ref[...] = x1_ref[...]

    a = lax.dot_general(h_ref[...], u_ref[...], _NT, preferred_element_type=F32)
    ws = []
    for c in range(g_ref.shape[0]):
        ac = a[:, c * LANE:(c + 1) * LANE]
        gelu = 0.5 * ac * (1.0 + lax.erf(ac * SQRT_HALF))
        ws.append((gelu * g_ref[c].astype(F32)).astype(BF16))
    w = jnp.concatenate(ws, axis=1)
    o_ref[...] += jnp.dot(w, v_ref[...], preferred_element_type=F32)


def _peer_dense(h2, x1, u, v, gates):
    n, d = h2.shape
    n_exp = u.shape[0]
    tm = _row_tile(n)
    te = 512
    ge = te // PEER_NKEYS
    row = lambda i, e: (i, 0)
    return pl.pallas_call(
        _peer_dense_kernel,
        grid=(n // tm, n_exp // te),
        in_specs=[
            pl.BlockSpec((tm, d), row),
            pl.BlockSpec((tm, d), row),
            pl.BlockSpec((te, d), lambda i, e: (e, 0)),
            pl.BlockSpec((te, d), lambda i, e: (e, 0)),
            pl.BlockSpec((ge, tm, PEER_NKEYS), lambda i, e: (e, i, 0)),
        ],
        out_specs=pl.BlockSpec((tm, d), row),
        out_shape=jax.ShapeDtypeStruct((n, d), F32),
        compiler_params=_cparams(("arbitrary", "arbitrary")),
        name="peer_dense",
    )(h2, x1, u, v, gates)


def kernel(x_prompt, x_sample, cache_k, cache_v, state_hgrn, page_table, meta_tokens, lb_logits, rel_bias,
           norm1_g, w_in, a_onorm_g, a_proj, b_qnorm_g, b_knorm_g, b_lambda, b_onorm_g, b_proj, w_out,
           norm2_g, peer_wq, peer_k1, peer_k2, peer_u, peer_v):
    assert w_in.shape[0] == 1, "one layer"
    nb, seq, d = x_prompt.shape
    nbs, t = x_sample.shape[:2]
    lp = seq + LANE
    n_p, n_s = nb * lp, nbs * t
    n = n_p + n_s

    front = jnp.concatenate([jnp.zeros((PAD_FRONT, d), F32), meta_tokens.astype(F32)], axis=0)
    xp = jnp.concatenate([jnp.broadcast_to(front[None], (nb, LANE, d)), x_prompt], axis=1)
    x = jnp.concatenate([xp.reshape(n_p, d), x_sample.reshape(n_s, d)], axis=0)

    bf = lambda w: w.astype(BF16)
    z = _rms_matmul(x, norm1_g[0], bf(w_in[0]), 1024)

    a_p, s_p = _hgrn(z, lb_logits, a_onorm_g[0], jnp.zeros((nb, A_HEADS, A_DK, A_DV), F32),
                     row0=0, nb=nb, t_len=lp, c=HGRN_CHUNK)
    a_s, s_s = _hgrn(z, lb_logits, a_onorm_g[0], state_hgrn[0], row0=n_p, nb=nbs, t_len=t, c=t)
    a_out = jnp.concatenate([a_p, a_s], axis=0)

    qn, kn, knb, vb = _qk_norm(z, b_qnorm_g[0], b_knorm_g[0])
    b_p = _attn_prompt(qn, knb, vb, rel_bias, b_lambda[0], b_onorm_g[0], nb=nb, lp=lp)
    w = B_HEADS * B_DV
    b_s = _attn_decode(qn[n_p:].astype(F32), kn[n_p:], z[n_p:, OFF_BV:OFF_BV + w], cache_k, cache_v, page_table,
                       rel_bias, b_lambda[0], b_onorm_g[0], nb=nbs, t=t)
    b_out = jnp.concatenate([b_p, b_s], axis=0)

    merged = _merge(a_out, b_out, z, bf(a_proj[0]), bf(b_proj[0]))
    x1, h2 = _out_proj(x, merged, bf(w_out[0]), norm2_g[0])

    scores_t = _peer_scores(h2, bf(peer_wq[0]), bf(peer_k1[0]), bf(peer_k2[0]))
    i1, i2, g = _peer_topk(scores_t)
    gates = _peer_gates(i1, i2, g)
    y = _peer_dense(h2, x1, bf(peer_u[0]), bf(peer_v[0]), gates)

    y_prompt = y[:n_p].reshape(nb, lp, d)[:, LANE:]
    y_sample = y[n_p:].reshape(nbs, t, d)
    k_prompt = kn[:n_p].reshape(nb, lp, B_HEADS, 2 * B_DH)[:, PAD_FRONT:][None]
    v_prompt = z[:n_p, OFF_BV:OFF_BV + w].reshape(nb, lp, B_HEADS, B_DV)[:, PAD_FRONT:][None]
    k_sample = kn[n_p:].reshape(nbs, t, B_HEADS, 2 * B_DH)[None]
    v_sample = z[n_p:, OFF_BV:OFF_BV + w].reshape(nbs, t, B_HEADS, B_DV)[None]
    return (y_prompt, y_sample, k_prompt, v_prompt, s_p[None], k_sample, v_sample, s_s[None])
```

```python
import functools
import math

import jax
import jax.numpy as jnp
from jax import lax
from jax.experimental import pallas as pl
from jax.experimental.pallas import tpu as pltpu

F32 = jnp.float32
BF16 = jnp.bfloat16
I32 = jnp.int32

LANE = 128
D_MODEL = 2048
N_META = 16
PAD_FRONT = LANE - N_META
A_HEADS = 8
A_DK = 128
A_DV = 128
B_HEADS = 8
B_DH = 64
B_DV = 128
N_BUCKETS = 32
MAX_DISTANCE = 128
PEER_HEADS = 8
PEER_NKEYS = 128
PEER_TOPK = 16
EPS = 1e-6
NEG_INF = -1e30
ATTN_SCALE = B_DH ** -0.5
SQRT_HALF = 0.7071067811865476
HGRN_CHUNK = 16
HGRN_HEADS_PER_STEP = 4
VMEM_LIMIT = 56 * 1024 * 1024

OFF_AQ, OFF_AF, OFF_AI, OFF_AG = 0, 1024, 2048, 3072
OFF_BQ, OFF_BK, OFF_BV = 4096, 5120, 6144
OFF_GA, OFF_GB = 7168, 9216
IN_TOTAL = 11264

_NT = (((1,), (1,)), ((), ()))
_TN = (((0,), (0,)), ((), ()))


def _cparams(sem):
    return pltpu.CompilerParams(dimension_semantics=sem, vmem_limit_bytes=VMEM_LIMIT)


def _row_tile(n, prefs=(768, 512, 384, 256, 128)):
    for t in prefs:
        if n % t == 0:
            return t
    raise ValueError(f"token count {n} is not a multiple of 128")


def _rms_matmul_kernel(x_ref, g_ref, w_ref, o_ref, h_ref):
    @pl.when(pl.program_id(1) == 0)
    def _():
        x = x_ref[...]
        ms = jnp.mean(x * x, axis=-1, keepdims=True)
        h_ref[...] = (x * lax.rsqrt(ms + EPS) * g_ref[...]).astype(BF16)

    o_ref[...] = jnp.dot(h_ref[...], w_ref[...], preferred_element_type=F32)


def _rms_matmul(x, g, w_bf16, tn):
    n, d = x.shape
    nc = w_bf16.shape[1]
    tm = _row_tile(n)
    return pl.pallas_call(
        _rms_matmul_kernel,
        grid=(n // tm, nc // tn),
        in_specs=[
            pl.BlockSpec((tm, d), lambda i, j: (i, 0)),
            pl.BlockSpec((1, d), lambda i, j: (0, 0)),
            pl.BlockSpec((d, tn), lambda i, j: (0, j)),
        ],
        out_specs=pl.BlockSpec((tm, tn), lambda i, j: (i, j)),
        out_shape=jax.ShapeDtypeStruct((n, nc), F32),
        scratch_shapes=[pltpu.VMEM((tm, d), BF16)],
        compiler_params=_cparams(("arbitrary", "arbitrary")),
        name="rms_in_proj",
    )(x, g.reshape(1, d), w_bf16)


def _hgrn_kernel(lbl_ref, gn_ref, q_ref, f_ref, i_ref, g_ref, s0_ref, o_ref, sfin_ref, st_scr, *, c, n_chunks, hp):
    for hh in range(hp):
        st_scr[hh] = s0_ref[0, hh].T
    lg = lbl_ref[...]
    e = jnp.exp(lg - jnp.max(lg, axis=0, keepdims=True))
    lb_all = e[0:1] / jnp.sum(e, axis=0, keepdims=True)
    gn = gn_ref[...]
    row = lax.broadcasted_iota(I32, (c, A_DK), 0)
    ones = jnp.ones((A_DK, A_DV), BF16)

    def chunk(ci, carry):
        r0 = pl.multiple_of(ci * c, c)
        for hh in range(hp):
            head_chunk(r0, hh)
        return carry

    def head_chunk(r0, hh):
        sl = slice(hh * A_DK, (hh + 1) * A_DK)
        lb = lb_all[:, sl]
        q = q_ref[pl.ds(r0, c), sl]
        zf = f_ref[pl.ds(r0, c), sl]
        v = i_ref[pl.ds(r0, c), sl]
        zg = g_ref[pl.ds(r0, c), sl]
        f = lb + (1.0 - lb) * jax.nn.sigmoid(zf)
        k = 1.0 - f
        b = jnp.log(f)
        sh = 1
        while sh < c:
            b = b + jnp.where(row >= sh, pltpu.roll(b, sh, 0), 0.0)
            sh *= 2
        ps = []
        for s in range(c):
            d = jnp.exp(jnp.where(row >= s, b - b[s:s + 1, :], NEG_INF))
            ps.append(q * d * k[s:s + 1, :])
        p = jnp.concatenate(ps, axis=0).astype(BF16)
        a = jnp.dot(p, ones, preferred_element_type=F32)
        o = a[0:c] * v[0:1, :]
        for s in range(1, c):
            o = o + a[s * c:(s + 1) * c] * v[s:s + 1, :]
        st = st_scr[hh]
        qe = (q * jnp.exp(b)).astype(BF16)
        o = o + lax.dot_general(qe, st.astype(BF16), _NT, preferred_element_type=F32)
        bc = b[c - 1:c, :]
        kt = (k * jnp.exp(bc - b)).astype(BF16)
        upd = lax.dot_general(v.astype(BF16), kt, _TN, preferred_element_type=F32)
        st_scr[hh] = st * jnp.exp(bc) + upd
        ms = jnp.mean(o * o, axis=-1, keepdims=True)
        o_ref[pl.ds(r0, c), sl] = o * lax.rsqrt(ms + EPS) * gn * (zg * jax.nn.sigmoid(zg))

    lax.fori_loop(0, n_chunks, chunk, 0, unroll=2 if n_chunks % 2 == 0 else 1)
    for hh in range(hp):
        sfin_ref[0, hh] = st_scr[hh].T


def _hgrn(z, lb_logits, gn, s0, *, row0, nb, t_len, c, hp):
    assert row0 % t_len == 0 and t_len % c == 0 and A_HEADS % hp == 0
    rb0 = row0 // t_len
    wb = hp * A_DK
    zspec = lambda off: pl.BlockSpec((t_len, wb), lambda b, h, off=off: (rb0 + b, off // wb + h))
    return pl.pallas_call(
        functools.partial(_hgrn_kernel, c=c, n_chunks=t_len // c, hp=hp),
        grid=(nb, A_HEADS // hp),
        in_specs=[
            pl.BlockSpec((lb_logits.shape[0], wb), lambda b, h: (0, h)),
            pl.BlockSpec((1, A_DV), lambda b, h: (0, 0)),
            zspec(OFF_AQ), zspec(OFF_AF), zspec(OFF_AI), zspec(OFF_AG),
            pl.BlockSpec((1, hp, A_DK, A_DV), lambda b, h: (b, h, 0, 0)),
        ],
        out_specs=[
            pl.BlockSpec((t_len, wb), lambda b, h: (b, h)),
            pl.BlockSpec((1, hp, A_DK, A_DV), lambda b, h: (b, h, 0, 0)),
        ],
        out_shape=[
            jax.ShapeDtypeStruct((nb * t_len, A_HEADS * A_DV), F32),
            jax.ShapeDtypeStruct((nb, A_HEADS, A_DK, A_DV), F32),
        ],
        scratch_shapes=[pltpu.VMEM((hp, A_DV, A_DK), F32)],
        compiler_params=_cparams(("arbitrary", "arbitrary")),
        name=f"hgrn_c{c}",
    )(lb_logits, gn.reshape(1, A_DV), z, z, z, z, s0)


def _qk_norm_kernel(zq_ref, zk_ref, zv_ref, qg_ref, kg_ref, qn_ref, qt_ref, kn_ref, knb_ref, vt_ref):
    r = lax.broadcasted_iota(I32, (LANE, LANE), 0) // B_DH
    cc = lax.broadcasted_iota(I32, (LANE, LANE), 1) // B_DH
    seg = jnp.where(r == cc, 1.0, 0.0).astype(BF16)

    def norm(z, g):
        sq = z * z
        hi = sq.astype(BF16)
        lo = (sq - hi.astype(F32)).astype(BF16)
        ss = jnp.dot(hi, seg, preferred_element_type=F32) + jnp.dot(lo, seg, preferred_element_type=F32)
        return z * lax.rsqrt(ss * (1.0 / B_DH) + EPS) * g

    qg = qg_ref[...]
    kg = kg_ref[...]
    for h in range(B_HEADS):
        sl = slice(h * LANE, (h + 1) * LANE)
        qn = norm(zq_ref[:, sl], qg) * ATTN_SCALE
        qn_ref[:, sl] = qn.astype(BF16)
        qt_ref[sl, :] = qn.T.astype(BF16)
        kn = norm(zk_ref[:, sl], kg)
        kn_ref[:, sl] = kn
        knb_ref[:, sl] = kn.astype(BF16)
        vt_ref[sl, :] = zv_ref[:, sl].T.astype(BF16)


def _qk_norm(z, qg, kg):
    n = z.shape[0]
    tm = _row_tile(n)
    w = B_HEADS * 2 * B_DH
    zspec = lambda off: pl.BlockSpec((tm, w), lambda i, off=off: (i, off // w))
    ospec = pl.BlockSpec((tm, w), lambda i: (i, 0))
    tspec = pl.BlockSpec((w, tm), lambda i: (0, i))
    gspec = pl.BlockSpec((1, LANE), lambda i: (0, 0))
    return pl.pallas_call(
        _qk_norm_kernel,
        grid=(n // tm,),
        in_specs=[zspec(OFF_BQ), zspec(OFF_BK), zspec(OFF_BV), gspec, gspec],
        out_specs=[ospec, tspec, ospec, ospec, tspec],
        out_shape=[
            jax.ShapeDtypeStruct((n, w), BF16),
            jax.ShapeDtypeStruct((w, n), BF16),
            jax.ShapeDtypeStruct((n, w), F32),
            jax.ShapeDtypeStruct((n, w), BF16),
            jax.ShapeDtypeStruct((w, n), BF16),
        ],
        compiler_params=_cparams(("arbitrary",)),
        name="qk_norm",
    )(z, z, z, jnp.tile(qg, 2).reshape(1, LANE), jnp.tile(kg, 2).reshape(1, LANE))


def _rel_bucket(dist):
    n = jnp.maximum(dist, 0)
    max_exact = N_BUCKETS // 2
    nf = jnp.maximum(n, 1).astype(F32)
    large = max_exact + (jnp.log(nf / max_exact) / math.log(MAX_DISTANCE / max_exact)
                         * (N_BUCKETS - max_exact)).astype(I32)
    large = jnp.minimum(large, N_BUCKETS - 1)
    return jnp.where(n < max_exact, n, large)


def _bias_from_bucket(bucket, rb_ref, h):
    val = jnp.zeros(bucket.shape, F32)
    for bk in range(N_BUCKETS):
        val = jnp.where(bucket == bk, rb_ref[bk, h], val)
    return val


def _diff_lambda(lam_ref):
    lp = lam_ref[...]
    lam_init = 0.8 - 0.6 * math.exp(-0.3 * 0)
    lam = (jnp.exp(jnp.sum(lp[0:1] * lp[1:2], axis=-1, keepdims=True))
           - jnp.exp(jnp.sum(lp[2:3] * lp[3:4], axis=-1, keepdims=True)) + lam_init)
    return lam, lam_init


def _split_maps(qh):
    lane = lax.broadcasted_iota(I32, qh.shape, 1)
    zero = jnp.zeros_like(qh)
    return jnp.concatenate([jnp.where(lane < B_DH, qh, zero), jnp.where(lane >= B_DH, qh, zero)], axis=0)


def _online_update(carry, s, vj):
    m, l, acc = carry
    m_new = jnp.maximum(m, jnp.max(s, axis=-1, keepdims=True))
    a = jnp.exp(m - m_new)
    p = jnp.exp(s - m_new)
    l = l * a + jnp.sum(p, axis=-1, keepdims=True)
    acc = acc * a + jnp.dot(p.astype(BF16), vj, preferred_element_type=F32)
    return m_new, l, acc


def _diff_finish(carry, t, lam, lam_init, og):
    _, l, acc = carry
    o = acc[0:t] / l[0:t] - lam * (acc[t:2 * t] / l[t:2 * t])
    ms = jnp.mean(o * o, axis=-1, keepdims=True)
    return o * lax.rsqrt(ms + EPS) * og * (1.0 - lam_init)


_BIAS_DIAG, _BIAS_NEAR, _BIAS_DIAG_PAD, _BIAS_NEAR_PAD, _BIAS_FAR_PAD = range(5)


def _attn_prompt_kernel(rb_ref, lam_ref, og_ref, qt_ref, k_ref, vt_ref, o_ref, bias_scr, q2_scr, m_scr, l_scr, acc_scr):
    i = pl.program_id(1)
    tq = qt_ref.shape[1]

    @pl.when((pl.program_id(0) == 0) & (i == 0))
    def _():
        r = lax.broadcasted_iota(I32, (tq, 2 * tq), 0)
        qi = lax.broadcasted_iota(I32, (tq, 2 * tq), 1) % tq
        in_pad = r < PAD_FRONT
        for blk in range(2):
            d = qi - r + blk * tq
            bucket = _rel_bucket(d)
            for h in range(B_HEADS):
                val = _bias_from_bucket(bucket, rb_ref, h)
                if blk == 0:
                    val = jnp.where(d >= 0, val, NEG_INF)
                bias_scr[_BIAS_DIAG + blk, h] = val
                bias_scr[_BIAS_DIAG_PAD + blk, h] = jnp.where(in_pad, NEG_INF, val)
        for h in range(B_HEADS):
            bias_scr[_BIAS_FAR_PAD, h] = jnp.where(in_pad, NEG_INF, rb_ref[N_BUCKETS - 1, h])

    row = lax.broadcasted_iota(I32, (LANE, tq), 0)
    for h in range(B_HEADS):
        qt = qt_ref[h * LANE:(h + 1) * LANE, :]
        zero = jnp.zeros_like(qt)
        q2_scr[h] = jnp.concatenate([jnp.where(row < B_DH, qt, zero), jnp.where(row >= B_DH, qt, zero)], axis=1)
    m_scr[...] = jnp.full(m_scr.shape, NEG_INF, F32)
    l_scr[...] = jnp.zeros(l_scr.shape, F32)
    acc_scr[...] = jnp.zeros(acc_scr.shape, F32)

    def step(j, bias_of):
        k0 = pl.multiple_of(j * tq, tq)
        for h in range(B_HEADS):
            sl = slice(h * LANE, (h + 1) * LANE)
            s = jnp.dot(k_ref[pl.ds(k0, tq), sl], q2_scr[h], preferred_element_type=F32) + bias_of(h)
            m = m_scr[h]
            m_new = jnp.maximum(m, jnp.max(s, axis=0, keepdims=True))
            a = jnp.exp(m - m_new)
            p = jnp.exp(s - m_new)
            m_scr[h] = m_new
            l_scr[h] = l_scr[h] * a + jnp.sum(p, axis=0, keepdims=True)
            acc_scr[h] = acc_scr[h] * a + jnp.dot(vt_ref[sl, pl.ds(k0, tq)], p.astype(BF16),
                                                  preferred_element_type=F32)

    first = jnp.where(i == 0, _BIAS_DIAG_PAD, jnp.where(i == 1, _BIAS_NEAR_PAD, _BIAS_FAR_PAD))
    step(0, lambda h: bias_scr[first, h])

    def far_body(j, carry):
        step(j, lambda h: rb_ref[N_BUCKETS - 1, h])
        return carry

    lax.fori_loop(1, jnp.maximum(i - 1, 1), far_body, 0)

    @pl.when(i >= 2)
    def _():
        step(i - 1, lambda h: bias_scr[_BIAS_NEAR, h])

    @pl.when(i >= 1)
    def _():
        step(i, lambda h: bias_scr[_BIAS_DIAG, h])

    lam, lam_init = _diff_lambda(lam_ref)
    og = og_ref[...]
    for h in range(B_HEADS):
        acc = acc_scr[h]
        l = l_scr[h]
        o = (acc[:, :tq] / l[:, :tq] - lam * (acc[:, tq:] / l[:, tq:])).T
        ms = jnp.mean(o * o, axis=-1, keepdims=True)
        o_ref[:, h * LANE:(h + 1) * LANE] = o * lax.rsqrt(ms + EPS) * og * (1.0 - lam_init)


def _attn_prompt(qt, knb, vt, rel_bias, b_lambda, og, *, nb, lp):
    tq = LANE
    nq = lp // tq
    w = B_HEADS * B_DV
    return pl.pallas_call(
        _attn_prompt_kernel,
        grid=(nb, nq),
        in_specs=[
            pl.BlockSpec(memory_space=pltpu.SMEM),
            pl.BlockSpec((4, B_DH), lambda b, i: (0, 0)),
            pl.BlockSpec((1, B_DV), lambda b, i: (0, 0)),
            pl.BlockSpec((w, tq), lambda b, i: (0, b * nq + i)),
            pl.BlockSpec((lp, w), lambda b, i: (b, 0)),
            pl.BlockSpec((w, lp), lambda b, i: (0, b)),
        ],
        out_specs=pl.BlockSpec((tq, w), lambda b, i: (b * nq + i, 0)),
        out_shape=jax.ShapeDtypeStruct((nb * lp, w), F32),
        scratch_shapes=[
            pltpu.VMEM((5, B_HEADS, tq, 2 * tq), F32),
            pltpu.VMEM((B_HEADS, LANE, 2 * tq), BF16),
            pltpu.VMEM((B_HEADS, 1, 2 * tq), F32),
            pltpu.VMEM((B_HEADS, 1, 2 * tq), F32),
            pltpu.VMEM((B_HEADS, B_DV, 2 * tq), F32),
        ],
        compiler_params=_cparams(("arbitrary", "arbitrary")),
        name="attn_prompt",
    )(rel_bias, b_lambda, og.reshape(1, B_DV), qt, knb, vt)


def _attn_decode_kernel(pt_ref, rb_ref, lam_ref, og_ref, q_ref, kn_ref, vn_ref, *rest, pages_per_step):
    kp = rest[:pages_per_step]
    vp = rest[pages_per_step:2 * pages_per_step]
    o_ref, bias_scr, qbd_scr, m_scr, l_scr, acc_scr = rest[2 * pages_per_step:]
    del pt_ref
    b = pl.program_id(0)
    p = pl.program_id(1)
    n_steps = pl.num_programs(1)
    t = q_ref.shape[0]
    page = kp[0].shape[1] // B_HEADS
    rows = 2 * t
    n_col = B_HEADS * rows
    pair = 2 * LANE
    n_pair = B_HEADS // 2

    def head_rows(ref, h):
        return ref[0, pl.ds(h, page, stride=B_HEADS), :].astype(BF16)

    @pl.when((b == 0) & (p == 0))
    def _():
        key = lax.broadcasted_iota(I32, (page, n_col), 0)
        col = lax.broadcasted_iota(I32, (page, n_col), 1)
        tok = col % t
        bk_last = _rel_bucket(page - key + tok)
        d_self = tok - key
        bk_self = _rel_bucket(d_self)
        v_last = jnp.zeros((page, n_col), F32)
        v_self = jnp.zeros((page, n_col), F32)
        v_far = jnp.zeros((page, n_col), F32)
        for h in range(B_HEADS):
            mine = col // rows == h
            v_last = jnp.where(mine, _bias_from_bucket(bk_last, rb_ref, h), v_last)
            v_self = jnp.where(mine, _bias_from_bucket(bk_self, rb_ref, h), v_self)
            v_far = jnp.where(mine, rb_ref[N_BUCKETS - 1, h], v_far)
        bias_scr[0] = v_last
        bias_scr[1] = jnp.where((d_self >= 0) & (key < t), v_self, NEG_INF)
        bias_scr[2] = v_far

    @pl.when(p == 0)
    def _():
        m_scr[...] = jnp.full(m_scr.shape, NEG_INF, F32)
        l_scr[...] = jnp.zeros(l_scr.shape, F32)
        acc_scr[...] = jnp.zeros(acc_scr.shape, F32)
        w = B_HEADS * LANE
        qrep = jnp.concatenate([q_ref[...]] * (n_col // t), axis=0)
        r = lax.broadcasted_iota(I32, (n_col, w), 0)
        f = lax.broadcasted_iota(I32, (n_col, w), 1)
        keep = (f // LANE == r // rows) & ((f % LANE) // B_DH == (r // t) % 2)
        qbd = jnp.where(keep, qrep, 0.0)
        for g in range(n_pair):
            for hh in range(2):
                c0 = g * pair + hh * LANE
                qbd_scr[g, hh * LANE:(hh + 1) * LANE, :] = qbd[:, c0:c0 + LANE].T.astype(BF16)

    far_row = bias_scr[2, 0:1, :]

    def key_scores(get_pair):
        s = jnp.dot(get_pair(0), qbd_scr[0], preferred_element_type=F32)
        for g in range(1, n_pair):
            s = s + jnp.dot(get_pair(g), qbd_scr[g], preferred_element_type=F32)
        return s

    def online_update(s, values):
        st = jnp.concatenate(s, axis=0)
        m = m_scr[...]
        m_new = jnp.maximum(m, jnp.max(st, axis=0, keepdims=True))
        a = jnp.exp(m - m_new)
        pt_ = jnp.exp(st - m_new)
        m_scr[...] = m_new
        l_scr[...] = l_scr[...] * a + jnp.sum(pt_, axis=0, keepdims=True)
        a_col = jnp.broadcast_to(a, (n_col, n_col)).T
        pq = [pt_[i * page:(i + 1) * page].T.astype(BF16) for i in range(len(s))]
        for h in range(B_HEADS):
            hs = slice(h * rows, (h + 1) * rows)
            pv = jnp.dot(pq[0][hs], values[0](h), preferred_element_type=F32)
            for i in range(1, len(s)):
                pv = pv + jnp.dot(pq[i][hs], values[i](h), preferred_element_type=F32)
            acc_scr[h] = acc_scr[h] * a_col[hs] + pv

    blocks = []
    for i in range(pages_per_step):
        s = key_scores(lambda g, i=i: jnp.concatenate([head_rows(kp[i], 2 * g), head_rows(kp[i], 2 * g + 1)], axis=1))
        if i == pages_per_step - 1:
            blocks.append(s + jnp.where(p == n_steps - 1, bias_scr[0], far_row))
        else:
            blocks.append(s + far_row)
    online_update(blocks, [lambda h, i=i: head_rows(vp[i], h) for i in range(pages_per_step)])

    @pl.when(p == n_steps - 1)
    def _():
        lam, lam_init = _diff_lambda(lam_ref)
        og = og_ref[...]
        kn = jnp.concatenate([kn_ref[...].astype(BF16), jnp.zeros((page - t, B_HEADS * LANE), BF16)], axis=0)
        s = key_scores(lambda g: kn[:, g * pair:(g + 1) * pair]) + bias_scr[1]
        zpad = jnp.zeros((page - t, LANE), BF16)
        online_update([s], [lambda h: jnp.concatenate([vn_ref[:, h * LANE:(h + 1) * LANE].astype(BF16), zpad], axis=0)])
        l_col = jnp.broadcast_to(l_scr[...], (n_col, n_col)).T
        for h in range(B_HEADS):
            hs = slice(h * rows, (h + 1) * rows)
            o_ref[:, h * LANE:(h + 1) * LANE] = _diff_finish((None, l_col[hs], acc_scr[h]), t, lam, lam_init, og)


def _attn_decode(qn, kn, vn, cache_k, cache_v, page_table, rel_bias, b_lambda, og, *, nb, t):
    n_pages = page_table.shape[1]
    n_phys, page = cache_k.shape[1], cache_k.shape[2]
    w = B_HEADS * B_DV
    n_col = B_HEADS * 2 * t
    assert n_col == LANE and page == LANE, "one score lane per (head, map, new token)"
    pps = 8
    while n_pages % pps:
        pps //= 2
    tok = pl.BlockSpec((t, w), lambda b, p, pt: (b, 0))
    ck = cache_k.reshape(n_phys, page * B_HEADS, B_DV)
    cv = cache_v.reshape(n_phys, page * B_HEADS, B_DV)
    pspec = lambda i: pl.BlockSpec((1, page * B_HEADS, B_DV), lambda b, p, pt, i=i: (pt[b, p * pps + i], 0, 0))
    grid_spec = pltpu.PrefetchScalarGridSpec(
        num_scalar_prefetch=1,
        grid=(nb, n_pages // pps),
        in_specs=[
            pl.BlockSpec(memory_space=pltpu.SMEM),
            pl.BlockSpec((4, B_DH), lambda b, p, pt: (0, 0)),
            pl.BlockSpec((1, B_DV), lambda b, p, pt: (0, 0)),
            tok, tok, tok,
        ] + [pspec(i) for i in range(pps)] + [pspec(i) for i in range(pps)],
        out_specs=pl.BlockSpec((t, w), lambda b, p, pt: (b, 0)),
        scratch_shapes=[
            pltpu.VMEM((3, page, n_col), F32),
            pltpu.VMEM((B_HEADS // 2, 2 * LANE, n_col), BF16),
            pltpu.VMEM((1, n_col), F32),
            pltpu.VMEM((1, n_col), F32),
            pltpu.VMEM((B_HEADS, 2 * t, B_DV), F32),
        ],
    )
    return pl.pallas_call(
        functools.partial(_attn_decode_kernel, pages_per_step=pps),
        grid_spec=grid_spec,
        out_shape=jax.ShapeDtypeStruct((nb * t, w), F32),
        compiler_params=_cparams(("arbitrary", "arbitrary")),
        name="attn_decode",
    )(page_table, rel_bias, b_lambda, og.reshape(1, B_DV), qn, kn, vn, *([ck] * pps), *([cv] * pps))


def _merge_kernel(a_ref, b_ref, za_ref, zb_ref, wa_ref, wb_ref, o_ref):
    ya = jnp.dot(a_ref[...].astype(BF16), wa_ref[...], preferred_element_type=F32)
    yb = jnp.dot(b_ref[...].astype(BF16), wb_ref[...], preferred_element_type=F32)
    o_ref[...] = (jax.nn.sigmoid(za_ref[...]) * ya + jax.nn.sigmoid(zb_ref[...]) * yb).astype(BF16)


def _merge(a_out, b_out, z, wa, wb):
    n, wa_in = a_out.shape
    d = wa.shape[1]
    tm, tn = _row_tile(n), 1024
    return pl.pallas_call(
        _merge_kernel,
        grid=(n // tm, d // tn),
        in_specs=[
            pl.BlockSpec((tm, wa_in), lambda i, j: (i, 0)),
            pl.BlockSpec((tm, b_out.shape[1]), lambda i, j: (i, 0)),
            pl.BlockSpec((tm, tn), lambda i, j: (i, OFF_GA // tn + j)),
            pl.BlockSpec((tm, tn), lambda i, j: (i, OFF_GB // tn + j)),
            pl.BlockSpec((wa_in, tn), lambda i, j: (0, j)),
            pl.BlockSpec((wb.shape[0], tn), lambda i, j: (0, j)),
        ],
        out_specs=pl.BlockSpec((tm, tn), lambda i, j: (i, j)),
        out_shape=jax.ShapeDtypeStruct((n, d), BF16),
        compiler_params=_cparams(("arbitrary", "arbitrary")),
        name="gated_merge",
    )(a_out, b_out, z, z, wa, wb)


def _out_proj_kernel(x_ref, m_ref, w_ref, g_ref, x1_ref, h2_ref):
    x1 = x_ref[...] + jnp.dot(m_ref[...], w_ref[...], preferred_element_type=F32)
    x1_ref[...] = x1
    ms = jnp.mean(x1 * x1, axis=-1, keepdims=True)
    h2_ref[...] = (x1 * lax.rsqrt(ms + EPS) * g_ref[...]).astype(BF16)


def _out_proj(x, merged, w_out, g2):
    n, d = x.shape
    tm = _row_tile(n, (256, 128))
    row = pl.BlockSpec((tm, d), lambda i: (i, 0))
    return pl.pallas_call(
        _out_proj_kernel,
        grid=(n // tm,),
        in_specs=[row, row, pl.BlockSpec((d, d), lambda i: (0, 0)), pl.BlockSpec((1, d), lambda i: (0, 0))],
        out_specs=[row, row],
        out_shape=[jax.ShapeDtypeStruct((n, d), F32), jax.ShapeDtypeStruct((n, d), BF16)],
        compiler_params=_cparams(("arbitrary",)),
        name="out_proj_norm2",
    )(x, merged, w_out, g2.reshape(1, d))


def _peer_scores_kernel(h_ref, wq_ref, k1_ref, k2_ref, s_ref):
    q = jnp.dot(h_ref[...], wq_ref[...], preferred_element_type=F32).astype(BF16)
    half = q.shape[1] // (2 * PEER_HEADS)
    for h in range(PEER_HEADS):
        for c, kr in enumerate((k1_ref, k2_ref)):
            qs = q[:, (2 * h + c) * half:(2 * h + c + 1) * half]
            s_ref[2 * h + c] = lax.dot_general(kr[h], qs, _NT, preferred_element_type=F32)


def _peer_scores(h2, wq, k1, k2):
    n, d = h2.shape
    tm = _row_tile(n, (256, 128))
    kspec = pl.BlockSpec(k1.shape, lambda i: (0, 0, 0))
    return pl.pallas_call(
        _peer_scores_kernel,
        grid=(n // tm,),
        in_specs=[pl.BlockSpec((tm, d), lambda i: (i, 0)), pl.BlockSpec(wq.shape, lambda i: (0, 0)), kspec, kspec],
        out_specs=pl.BlockSpec((2 * PEER_HEADS, PEER_NKEYS, tm), lambda i: (0, 0, i)),
        out_shape=jax.ShapeDtypeStruct((2 * PEER_HEADS, PEER_NKEYS, n), F32),
        compiler_params=_cparams(("arbitrary",)),
        name="peer_scores",
    )(h2, wq, k1, k2)


def _cand_layout():
    k = PEER_TOPK
    groups = [(0, k)] + [(j, 8) for j in range(1, 8)]
    pos = []
    for j, rows in groups:
        for l in range(rows):
            pos.append(j * k + l if (j + 1) * (l + 1) <= k else 1 << 20)
    pos += [j * k for j in range(8, k)]
    return groups, pos


def _topk_rows(s, k):
    n_rows = s.shape[0]
    row = lax.broadcasted_iota(I32, s.shape, 0)
    vals, idxs = [], []
    for _ in range(k):
        m = jnp.max(s, axis=0, keepdims=True)
        idx = jnp.min(jnp.where(s == m, row, n_rows), axis=0, keepdims=True)
        vals.append(m)
        idxs.append(idx)
        s = jnp.where(row == idx, NEG_INF, s)
    return jnp.concatenate(vals, axis=0), jnp.concatenate(idxs, axis=0)


def _peer_topk_kernel(s_ref, pos_ref, i1_ref, i2_ref, g_ref):
    k = PEER_TOPK
    groups, _ = _cand_layout()
    pos = pos_ref[...]
    big = 1 << 20
    e_rows, g_rows = [], []
    for h in range(PEER_HEADS):
        v1, i1 = _topk_rows(s_ref[2 * h], k)
        v2, i2 = _topk_rows(s_ref[2 * h + 1], k)
        cand, cidx = [], []
        for j, rows in groups:
            cand.append(v1[j:j + 1] + v2[0:rows])
            cidx.append(i1[j:j + 1] * PEER_NKEYS + i2[0:rows])
        cand.append(v1[8:k] + v2[0:1])
        cidx.append(i1[8:k] * PEER_NKEYS + i2[0:1])
        cand = jnp.where(pos < big, jnp.concatenate(cand, axis=0), NEG_INF)
        cidx = jnp.concatenate(cidx, axis=0)
        sc, ex = [], []
        for _ in range(k):
            m = jnp.max(cand, axis=0, keepdims=True)
            pidx = jnp.min(jnp.where(cand == m, pos, big), axis=0, keepdims=True)
            sel = pos == pidx
            sc.append(m)
            ex.append(jnp.max(jnp.where(sel, cidx, -1), axis=0, keepdims=True))
            cand = jnp.where(sel, NEG_INF, cand)
        sc = jnp.concatenate(sc, axis=0)
        ew = jnp.exp(sc - sc[0:1])
        g_rows.append(ew / jnp.sum(ew, axis=0, keepdims=True))
        e_rows.append(jnp.concatenate(ex, axis=0))
    e_all = jnp.concatenate(e_rows, axis=0)
    g_all = jnp.concatenate(g_rows, axis=0)
    i1_ref[...] = (e_all >> 7).astype(F32).T
    i2_ref[...] = (e_all & (PEER_NKEYS - 1)).astype(F32).T
    g_ref[...] = g_all.T


def _peer_topk(scores_t):
    n = scores_t.shape[2]
    tl = LANE
    _, pos = _cand_layout()
    pos_tab = jnp.broadcast_to(jnp.asarray(pos, I32)[:, None], (len(pos), tl))
    nsel = PEER_HEADS * PEER_TOPK
    ospec = pl.BlockSpec((tl, nsel), lambda i: (i, 0))
    oshape = jax.ShapeDtypeStruct((n, nsel), F32)
    return pl.pallas_call(
        _peer_topk_kernel,
        grid=(n // tl,),
        in_specs=[pl.BlockSpec((2 * PEER_HEADS, PEER_NKEYS, tl), lambda i: (0, 0, i)),
                  pl.BlockSpec(pos_tab.shape, lambda i: (0, 0))],
        out_specs=[ospec, ospec, ospec],
        out_shape=[oshape, oshape, oshape],
        compiler_params=_cparams(("arbitrary",)),
        name="peer_topk",
    )(scores_t, pos_tab)


def _peer_gates_kernel(i1_ref, i2_ref, g_ref, o_ref, x_scr):
    tb = i1_ref.shape[0]
    grp = 16
    sub = lax.broadcasted_iota(I32, (PEER_NKEYS, LANE), 0).astype(F32)

    def group(gi, carry):
        r0 = pl.multiple_of(gi * grp, grp)
        for t in range(grp):
            i1 = i1_ref[pl.ds(r0 + t, 1), :]
            i2 = i2_ref[pl.ds(r0 + t, 1), :]
            g = g_ref[pl.ds(r0 + t, 1), :]
            p1 = jnp.where(i1 == sub, g, 0.0).astype(BF16)
            p2 = jnp.where(i2 == sub, 1.0, 0.0).astype(BF16)
            x_scr[t * PEER_NKEYS:(t + 1) * PEER_NKEYS, :] = lax.dot_general(
                p1, p2, _NT, preferred_element_type=F32)

        def relayout(i1, cr):
            o_ref[i1, pl.ds(r0, grp), :] = x_scr[pl.ds(i1, grp, stride=PEER_NKEYS), :].astype(BF16)
            return cr

        lax.fori_loop(0, PEER_NKEYS, relayout, 0, unroll=32)
        return carry

    lax.fori_loop(0, tb // grp, group, 0)


def _peer_gates(i1, i2, g):
    n, nsel = i1.shape
    tb = LANE
    ispec = pl.BlockSpec((tb, nsel), lambda i: (i, 0))
    return pl.pallas_call(
        _peer_gates_kernel,
        grid=(n // tb,),
        in_specs=[ispec, ispec, ispec],
        out_specs=pl.BlockSpec((PEER_NKEYS, tb, PEER_NKEYS), lambda i: (0, i, 0)),
        out_shape=jax.ShapeDtypeStruct((PEER_NKEYS, n, PEER_NKEYS), BF16),
        scratch_shapes=[pltpu.VMEM((16 * PEER_NKEYS, PEER_NKEYS), F32)],
        compiler_params=_cparams(("arbitrary",)),
        name="peer_gates",
    )(i1, i2, g)


def _peer_dense_kernel(h_ref, x1_ref, u_ref, v_ref, g_ref, o_ref):
    @pl.when(pl.program_id(1) == 0)
    def _():
        o_ref[...] = x1_ref[...]

    a = lax.dot_general(h_ref[...], u_ref[...], _NT, preferred_element_type=F32)
    ws = []
    for c in range(g_ref.shape[0]):
        ac = a[:, c * LANE:(c + 1) * LANE]
        gelu = 0.5 * ac * (1.0 + lax.erf(ac * SQRT_HALF))
        ws.append((gelu * g_ref[c].astype(F32)).astype(BF16))
    w = jnp.concatenate(ws, axis=1)
    o_ref[...] += jnp.dot(w, v_ref[...], preferred_element_type=F32)


def _peer_dense(h2, x1, u, v, gates):
    n, d = h2.shape
    n_exp = u.shape[0]
    tm = _row_tile(n)
    te = 512
    ge = te // PEER_NKEYS
    row = lambda i, e: (i, 0)
    return pl.pallas_call(
        _peer_dense_kernel,
        grid=(n // tm, n_exp // te),
        in_specs=[
            pl.BlockSpec((tm, d), row),
            pl.BlockSpec((tm, d), row),
            pl.BlockSpec((te, d), lambda i, e: (e, 0)),
            pl.BlockSpec((te, d), lambda i, e: (e, 0)),
            pl.BlockSpec((ge, tm, PEER_NKEYS), lambda i, e: (e, i, 0)),
        ],
        out_specs=pl.BlockSpec((tm, d), row),
        out_shape=jax.ShapeDtypeStruct((n, d), F32),
        compiler_params=_cparams(("arbitrary", "arbitrary")),
        name="peer_dense",
    )(h2, x1, u, v, gates)


def kernel(x_prompt, x_sample, cache_k, cache_v, state_hgrn, page_table, meta_tokens, lb_logits, rel_bias,
           norm1_g, w_in, a_onorm_g, a_proj, b_qnorm_g, b_knorm_g, b_lambda, b_onorm_g, b_proj, w_out,
           norm2_g, peer_wq, peer_k1, peer_k2, peer_u, peer_v):
    assert w_in.shape[0] == 1, "one layer"
    nb, seq, d = x_prompt.shape
    nbs, t = x_sample.shape[:2]
    lp = seq + LANE
    n_p, n_s = nb * lp, nbs * t
    n = n_p + n_s

    front = jnp.concatenate([jnp.zeros((PAD_FRONT, d), F32), meta_tokens.astype(F32)], axis=0)
    xp = jnp.concatenate([jnp.broadcast_to(front[None], (nb, LANE, d)), x_prompt], axis=1)
    x = jnp.concatenate([xp.reshape(n_p, d), x_sample.reshape(n_s, d)], axis=0)

    bf = lambda w: w.astype(BF16)
    z = _rms_matmul(x, norm1_g[0], bf(w_in[0]), 1024)

    a_p, s_p = _hgrn(z, lb_logits, a_onorm_g[0], jnp.zeros((nb, A_HEADS, A_DK, A_DV), F32),
                     row0=0, nb=nb, t_len=lp, c=HGRN_CHUNK, hp=HGRN_HEADS_PER_STEP)
    a_s, s_s = _hgrn(z, lb_logits, a_onorm_g[0], state_hgrn[0], row0=n_p, nb=nbs, t_len=t, c=t, hp=A_HEADS)
    a_out = jnp.concatenate([a_p, a_s], axis=0)

    qn, qt, kn, knb, vt = _qk_norm(z, b_qnorm_g[0], b_knorm_g[0])
    b_p = _attn_prompt(qt, knb, vt, rel_bias, b_lambda[0], b_onorm_g[0], nb=nb, lp=lp)
    w = B_HEADS * B_DV
    b_s = _attn_decode(qn[n_p:].astype(F32), kn[n_p:], z[n_p:, OFF_BV:OFF_BV + w], cache_k, cache_v, page_table,
                       rel_bias, b_lambda[0], b_onorm_g[0], nb=nbs, t=t)
    b_out = jnp.concatenate([b_p, b_s], axis=0)

    merged = _merge(a_out, b_out, z, bf(a_proj[0]), bf(b_proj[0]))
    x1, h2 = _out_proj(x, merged, bf(w_out[0]), norm2_g[0])

    scores_t = _peer_scores(h2, bf(peer_wq[0]), bf(peer_k1[0]), bf(peer_k2[0]))
    i1, i2, g = _peer_topk(scores_t)
    gates = _peer_gates(i1, i2, g)
    y = _peer_dense(h2, x1, bf(peer_u[0]), bf(peer_v[0]), gates)

    y_prompt = y[:n_p].reshape(nb, lp, d)[:, LANE:]
    y_sample = y[n_p:].reshape(nbs, t, d)
    k_prompt = kn[:n_p].reshape(nb, lp, B_HEADS, 2 * B_DH)[:, PAD_FRONT:][None]
    v_prompt = z[:n_p, OFF_BV:OFF_BV + w].reshape(nb, lp, B_HEADS, B_DV)[:, PAD_FRONT:][None]
    k_sample = kn[n_p:].reshape(nbs, t, B_HEADS, 2 * B_DH)[None]
    v_sample = z[n_p:, OFF_BV:OFF_BV + w].reshape(nbs, t, B_HEADS, B_DV)[None]
    return (y_prompt, y_sample, k_prompt, v_prompt, s_p[None], k_sample, v_sample, s_s[None])
```

```python
import functools
import math

import jax
import jax.numpy as jnp
from jax import lax
from jax.experimental import pallas as pl
from jax.experimental.pallas import tpu as pltpu

F32 = jnp.float32
BF16 = jnp.bfloat16
I32 = jnp.int32

LANE = 128
D_MODEL = 2048
N_META = 16
PAD_FRONT = LANE - N_META
A_HEADS = 8
A_DK = 128
A_DV = 128
B_HEADS = 8
B_DH = 64
B_DV = 128
N_BUCKETS = 32
MAX_DISTANCE = 128
PEER_HEADS = 8
PEER_NKEYS = 128
PEER_TOPK = 16
EPS = 1e-6
NEG_INF = -1e30
ATTN_SCALE = B_DH ** -0.5
SQRT_HALF = 0.7071067811865476
HGRN_CHUNK = 16
HGRN_HEADS_PER_STEP = 4
VMEM_LIMIT = 56 * 1024 * 1024

OFF_AQ, OFF_AF, OFF_AI, OFF_AG = 0, 1024, 2048, 3072
OFF_BQ, OFF_BK, OFF_BV = 4096, 5120, 6144
OFF_GA, OFF_GB = 7168, 9216
IN_TOTAL = 11264

_NT = (((1,), (1,)), ((), ()))
_TN = (((0,), (0,)), ((), ()))


def _cparams(sem):
    return pltpu.CompilerParams(dimension_semantics=sem, vmem_limit_bytes=VMEM_LIMIT)


def _row_tile(n, prefs=(768, 512, 384, 256, 128)):
    for t in prefs:
        if n % t == 0:
            return t
    raise ValueError(f"token count {n} is not a multiple of 128")


def _rms_matmul_kernel(x_ref, g_ref, w_ref, o_ref, h_ref):
    @pl.when(pl.program_id(1) == 0)
    def _():
        x = x_ref[...]
        ms = jnp.mean(x * x, axis=-1, keepdims=True)
        h_ref[...] = (x * lax.rsqrt(ms + EPS) * g_ref[...]).astype(BF16)

    o_ref[...] = jnp.dot(h_ref[...], w_ref[...], preferred_element_type=F32)


def _rms_matmul(x, g, w_bf16, tn):
    n, d = x.shape
    nc = w_bf16.shape[1]
    tm = _row_tile(n)
    return pl.pallas_call(
        _rms_matmul_kernel,
        grid=(n // tm, nc // tn),
        in_specs=[
            pl.BlockSpec((tm, d), lambda i, j: (i, 0)),
            pl.BlockSpec((1, d), lambda i, j: (0, 0)),
            pl.BlockSpec((d, tn), lambda i, j: (0, j)),
        ],
        out_specs=pl.BlockSpec((tm, tn), lambda i, j: (i, j)),
        out_shape=jax.ShapeDtypeStruct((n, nc), F32),
        scratch_shapes=[pltpu.VMEM((tm, d), BF16)],
        compiler_params=_cparams(("arbitrary", "arbitrary")),
        name="rms_in_proj",
    )(x, g.reshape(1, d), w_bf16)


def _hgrn_kernel(lbl_ref, gn_ref, q_ref, f_ref, i_ref, g_ref, s0_ref, o_ref, sfin_ref, st_scr, *, c, n_chunks, hp):
    for hh in range(hp):
        st_scr[hh] = s0_ref[0, hh].T
    lg = lbl_ref[...]
    e = jnp.exp(lg - jnp.max(lg, axis=0, keepdims=True))
    lb_all = e[0:1] / jnp.sum(e, axis=0, keepdims=True)
    gn = gn_ref[...]
    row = lax.broadcasted_iota(I32, (c, A_DK), 0)
    ones = jnp.ones((A_DK, A_DV), BF16)

    def chunk(ci, carry):
        r0 = pl.multiple_of(ci * c, c)
        for hh in range(hp):
            head_chunk(r0, hh)
        return carry

    def head_chunk(r0, hh):
        sl = slice(hh * A_DK, (hh + 1) * A_DK)
        lb = lb_all[:, sl]
        q = q_ref[pl.ds(r0, c), sl]
        zf = f_ref[pl.ds(r0, c), sl]
        v = i_ref[pl.ds(r0, c), sl]
        zg = g_ref[pl.ds(r0, c), sl]
        f = lb + (1.0 - lb) * jax.nn.sigmoid(zf)
        k = 1.0 - f
        b = jnp.log(f)
        sh = 1
        while sh < c:
            b = b + jnp.where(row >= sh, pltpu.roll(b, sh, 0), 0.0)
            sh *= 2
        sub = 8
        ps = []
        for s in range(c):
            t0 = s // sub * sub
            d = jnp.exp(jnp.where(row[t0:] >= s, b[t0:] - b[s:s + 1, :], NEG_INF))
            ps.append(q[t0:] * d * k[s:s + 1, :])
        p = jnp.concatenate(ps, axis=0).astype(BF16)
        a = jnp.dot(p, ones, preferred_element_type=F32)
        o_groups = [None] * (c // sub)
        off = 0
        for s in range(c):
            t0 = s // sub * sub
            for gi in range(t0 // sub, c // sub):
                r0_ = off + gi * sub - t0
                piece = a[r0_:r0_ + sub] * v[s:s + 1, :]
                o_groups[gi] = piece if o_groups[gi] is None else o_groups[gi] + piece
            off += c - t0
        o = jnp.concatenate(o_groups, axis=0)
        st = st_scr[hh]
        qe = (q * jnp.exp(b)).astype(BF16)
        o = o + lax.dot_general(qe, st.astype(BF16), _NT, preferred_element_type=F32)
        bc = b[c - 1:c, :]
        kt = (k * jnp.exp(bc - b)).astype(BF16)
        upd = lax.dot_general(v.astype(BF16), kt, _TN, preferred_element_type=F32)
        st_scr[hh] = st * jnp.exp(bc) + upd
        ms = jnp.mean(o * o, axis=-1, keepdims=True)
        o_ref[pl.ds(r0, c), sl] = o * lax.rsqrt(ms + EPS) * gn * (zg * jax.nn.sigmoid(zg))

    lax.fori_loop(0, n_chunks, chunk, 0, unroll=4 if n_chunks % 4 == 0 else 1)
    for hh in range(hp):
        sfin_ref[0, hh] = st_scr[hh].T


def _hgrn(z, lb_logits, gn, s0, *, row0, nb, t_len, c, hp):
    assert row0 % t_len == 0 and t_len % c == 0 and A_HEADS % hp == 0
    rb0 = row0 // t_len
    wb = hp * A_DK
    zspec = lambda off: pl.BlockSpec((t_len, wb), lambda b, h, off=off: (rb0 + b, off // wb + h))
    return pl.pallas_call(
        functools.partial(_hgrn_kernel, c=c, n_chunks=t_len // c, hp=hp),
        grid=(nb, A_HEADS // hp),
        in_specs=[
            pl.BlockSpec((lb_logits.shape[0], wb), lambda b, h: (0, h)),
            pl.BlockSpec((1, A_DV), lambda b, h: (0, 0)),
            zspec(OFF_AQ), zspec(OFF_AF), zspec(OFF_AI), zspec(OFF_AG),
            pl.BlockSpec((1, hp, A_DK, A_DV), lambda b, h: (b, h, 0, 0)),
        ],
        out_specs=[
            pl.BlockSpec((t_len, wb), lambda b, h: (b, h)),
            pl.BlockSpec((1, hp, A_DK, A_DV), lambda b, h: (b, h, 0, 0)),
        ],
        out_shape=[
            jax.ShapeDtypeStruct((nb * t_len, A_HEADS * A_DV), F32),
            jax.ShapeDtypeStruct((nb, A_HEADS, A_DK, A_DV), F32),
        ],
        scratch_shapes=[pltpu.VMEM((hp, A_DV, A_DK), F32)],
        compiler_params=_cparams(("arbitrary", "arbitrary")),
        name=f"hgrn_c{c}",
    )(lb_logits, gn.reshape(1, A_DV), z, z, z, z, s0)


def _qk_norm_kernel(zq_ref, zk_ref, zv_ref, qg_ref, kg_ref, qn_ref, qt_ref, kn_ref, knb_ref, vt_ref):
    r = lax.broadcasted_iota(I32, (LANE, LANE), 0) // B_DH
    cc = lax.broadcasted_iota(I32, (LANE, LANE), 1) // B_DH
    seg = jnp.where(r == cc, 1.0, 0.0).astype(BF16)

    def norm(z, g):
        sq = z * z
        hi = sq.astype(BF16)
        lo = (sq - hi.astype(F32)).astype(BF16)
        ss = jnp.dot(hi, seg, preferred_element_type=F32) + jnp.dot(lo, seg, preferred_element_type=F32)
        return z * lax.rsqrt(ss * (1.0 / B_DH) + EPS) * g

    qg = qg_ref[...]
    kg = kg_ref[...]
    for h in range(B_HEADS):
        sl = slice(h * LANE, (h + 1) * LANE)
        qn = norm(zq_ref[:, sl], qg) * ATTN_SCALE
        qn_ref[:, sl] = qn.astype(BF16)
        qt_ref[sl, :] = qn.T.astype(BF16)
        kn = norm(zk_ref[:, sl], kg)
        kn_ref[:, sl] = kn
        knb_ref[:, sl] = kn.astype(BF16)
        vt_ref[sl, :] = zv_ref[:, sl].T.astype(BF16)


def _qk_norm(z, qg, kg):
    n = z.shape[0]
    tm = _row_tile(n)
    w = B_HEADS * 2 * B_DH
    zspec = lambda off: pl.BlockSpec((tm, w), lambda i, off=off: (i, off // w))
    ospec = pl.BlockSpec((tm, w), lambda i: (i, 0))
    tspec = pl.BlockSpec((w, tm), lambda i: (0, i))
    gspec = pl.BlockSpec((1, LANE), lambda i: (0, 0))
    return pl.pallas_call(
        _qk_norm_kernel,
        grid=(n // tm,),
        in_specs=[zspec(OFF_BQ), zspec(OFF_BK), zspec(OFF_BV), gspec, gspec],
        out_specs=[ospec, tspec, ospec, ospec, tspec],
        out_shape=[
            jax.ShapeDtypeStruct((n, w), BF16),
            jax.ShapeDtypeStruct((w, n), BF16),
            jax.ShapeDtypeStruct((n, w), F32),
            jax.ShapeDtypeStruct((n, w), BF16),
            jax.ShapeDtypeStruct((w, n), BF16),
        ],
        compiler_params=_cparams(("arbitrary",)),
        name="qk_norm",
    )(z, z, z, jnp.tile(qg, 2).reshape(1, LANE), jnp.tile(kg, 2).reshape(1, LANE))


def _rel_bucket(dist):
    n = jnp.maximum(dist, 0)
    max_exact = N_BUCKETS // 2
    nf = jnp.maximum(n, 1).astype(F32)
    large = max_exact + (jnp.log(nf / max_exact) / math.log(MAX_DISTANCE / max_exact)
                         * (N_BUCKETS - max_exact)).astype(I32)
    large = jnp.minimum(large, N_BUCKETS - 1)
    return jnp.where(n < max_exact, n, large)


def _bias_from_bucket(bucket, rb_ref, h):
    val = jnp.zeros(bucket.shape, F32)
    for bk in range(N_BUCKETS):
        val = jnp.where(bucket == bk, rb_ref[bk, h], val)
    return val


def _diff_lambda(lam_ref):
    lp = lam_ref[...]
    lam_init = 0.8 - 0.6 * math.exp(-0.3 * 0)
    lam = (jnp.exp(jnp.sum(lp[0:1] * lp[1:2], axis=-1, keepdims=True))
           - jnp.exp(jnp.sum(lp[2:3] * lp[3:4], axis=-1, keepdims=True)) + lam_init)
    return lam, lam_init


def _split_maps(qh):
    lane = lax.broadcasted_iota(I32, qh.shape, 1)
    zero = jnp.zeros_like(qh)
    return jnp.concatenate([jnp.where(lane < B_DH, qh, zero), jnp.where(lane >= B_DH, qh, zero)], axis=0)


def _online_update(carry, s, vj):
    m, l, acc = carry
    m_new = jnp.maximum(m, jnp.max(s, axis=-1, keepdims=True))
    a = jnp.exp(m - m_new)
    p = jnp.exp(s - m_new)
    l = l * a + jnp.sum(p, axis=-1, keepdims=True)
    acc = acc * a + jnp.dot(p.astype(BF16), vj, preferred_element_type=F32)
    return m_new, l, acc


def _diff_finish(carry, t, lam, lam_init, og):
    _, l, acc = carry
    o = acc[0:t] / l[0:t] - lam * (acc[t:2 * t] / l[t:2 * t])
    ms = jnp.mean(o * o, axis=-1, keepdims=True)
    return o * lax.rsqrt(ms + EPS) * og * (1.0 - lam_init)


_BIAS_DIAG, _BIAS_NEAR, _BIAS_DIAG_PAD, _BIAS_NEAR_PAD, _BIAS_FAR_PAD = range(5)


def _attn_prompt_kernel(rb_ref, lam_ref, og_ref, qt_ref, k_ref, vt_ref, o_ref, bias_scr, q2_scr, m_scr, l_scr, acc_scr):
    i = pl.program_id(1)
    tq = qt_ref.shape[1]

    @pl.when((pl.program_id(0) == 0) & (i == 0))
    def _():
        r = lax.broadcasted_iota(I32, (tq, 2 * tq), 0)
        qi = lax.broadcasted_iota(I32, (tq, 2 * tq), 1) % tq
        in_pad = r < PAD_FRONT
        for blk in range(2):
            d = qi - r + blk * tq
            bucket = _rel_bucket(d)
            for h in range(B_HEADS):
                val = _bias_from_bucket(bucket, rb_ref, h)
                if blk == 0:
                    val = jnp.where(d >= 0, val, NEG_INF)
                bias_scr[_BIAS_DIAG + blk, h] = val
                bias_scr[_BIAS_DIAG_PAD + blk, h] = jnp.where(in_pad, NEG_INF, val)
        for h in range(B_HEADS):
            bias_scr[_BIAS_FAR_PAD, h] = jnp.where(in_pad, NEG_INF, rb_ref[N_BUCKETS - 1, h])

    row = lax.broadcasted_iota(I32, (LANE, tq), 0)
    for h in range(B_HEADS):
        qt = qt_ref[h * LANE:(h + 1) * LANE, :]
        zero = jnp.zeros_like(qt)
        q2_scr[h] = jnp.concatenate([jnp.where(row < B_DH, qt, zero), jnp.where(row >= B_DH, qt, zero)], axis=1)
    m_scr[...] = jnp.full(m_scr.shape, NEG_INF, F32)
    l_scr[...] = jnp.zeros(l_scr.shape, F32)
    acc_scr[...] = jnp.zeros(acc_scr.shape, F32)

    def step(j, bias_of):
        k0 = pl.multiple_of(j * tq, tq)
        for h in range(B_HEADS):
            sl = slice(h * LANE, (h + 1) * LANE)
            s = jnp.dot(k_ref[pl.ds(k0, tq), sl], q2_scr[h], preferred_element_type=F32) + bias_of(h)
            m = m_scr[h]
            m_new = jnp.maximum(m, jnp.max(s, axis=0, keepdims=True))
            a = jnp.exp(m - m_new)
            p = jnp.exp(s - m_new)
            m_scr[h] = m_new
            l_scr[h] = l_scr[h] * a + jnp.sum(p, axis=0, keepdims=True)
            acc_scr[h] = acc_scr[h] * a + jnp.dot(vt_ref[sl, pl.ds(k0, tq)], p.astype(BF16),
                                                  preferred_element_type=F32)

    first = jnp.where(i == 0, _BIAS_DIAG_PAD, jnp.where(i == 1, _BIAS_NEAR_PAD, _BIAS_FAR_PAD))
    step(0, lambda h: bias_scr[first, h])

    def far_body(j, carry):
        step(j, lambda h: rb_ref[N_BUCKETS - 1, h])
        return carry

    lax.fori_loop(1, jnp.maximum(i - 1, 1), far_body, 0)

    @pl.when(i >= 2)
    def _():
        step(i - 1, lambda h: bias_scr[_BIAS_NEAR, h])

    @pl.when(i >= 1)
    def _():
        step(i, lambda h: bias_scr[_BIAS_DIAG, h])

    lam, lam_init = _diff_lambda(lam_ref)
    og = og_ref[...]
    for h in range(B_HEADS):
        acc = acc_scr[h]
        l = l_scr[h]
        o = (acc[:, :tq] / l[:, :tq] - lam * (acc[:, tq:] / l[:, tq:])).T
        ms = jnp.mean(o * o, axis=-1, keepdims=True)
        o_ref[:, h * LANE:(h + 1) * LANE] = o * lax.rsqrt(ms + EPS) * og * (1.0 - lam_init)


def _attn_prompt(qt, knb, vt, rel_bias, b_lambda, og, *, nb, lp):
    tq = LANE
    nq = lp // tq
    w = B_HEADS * B_DV
    return pl.pallas_call(
        _attn_prompt_kernel,
        grid=(nb, nq),
        in_specs=[
            pl.BlockSpec(memory_space=pltpu.SMEM),
            pl.BlockSpec((4, B_DH), lambda b, i: (0, 0)),
            pl.BlockSpec((1, B_DV), lambda b, i: (0, 0)),
            pl.BlockSpec((w, tq), lambda b, i: (0, b * nq + i)),
            pl.BlockSpec((lp, w), lambda b, i: (b, 0)),
            pl.BlockSpec((w, lp), lambda b, i: (0, b)),
        ],
        out_specs=pl.BlockSpec((tq, w), lambda b, i: (b * nq + i, 0)),
        out_shape=jax.ShapeDtypeStruct((nb * lp, w), F32),
        scratch_shapes=[
            pltpu.VMEM((5, B_HEADS, tq, 2 * tq), F32),
            pltpu.VMEM((B_HEADS, LANE, 2 * tq), BF16),
            pltpu.VMEM((B_HEADS, 1, 2 * tq), F32),
            pltpu.VMEM((B_HEADS, 1, 2 * tq), F32),
            pltpu.VMEM((B_HEADS, B_DV, 2 * tq), F32),
        ],
        compiler_params=_cparams(("arbitrary", "arbitrary")),
        name="attn_prompt",
    )(rel_bias, b_lambda, og.reshape(1, B_DV), qt, knb, vt)


def _attn_decode_kernel(pt_ref, rb_ref, lam_ref, og_ref, q_ref, kn_ref, vn_ref, *rest, pages_per_step):
    kp = rest[:pages_per_step]
    vp = rest[pages_per_step:2 * pages_per_step]
    o_ref, bias_scr, qbd_scr, m_scr, l_scr, acc_scr = rest[2 * pages_per_step:]
    del pt_ref
    b = pl.program_id(0)
    p = pl.program_id(1)
    n_steps = pl.num_programs(1)
    t = q_ref.shape[0]
    page = kp[0].shape[1] // B_HEADS
    rows = 2 * t
    n_col = B_HEADS * rows
    pair = 2 * LANE
    n_pair = B_HEADS // 2

    def head_rows(ref, h):
        return ref[0, pl.ds(h, page, stride=B_HEADS), :].astype(BF16)

    @pl.when((b == 0) & (p == 0))
    def _():
        key = lax.broadcasted_iota(I32, (page, n_col), 0)
        col = lax.broadcasted_iota(I32, (page, n_col), 1)
        tok = col % t
        bk_last = _rel_bucket(page - key + tok)
        d_self = tok - key
        bk_self = _rel_bucket(d_self)
        v_last = jnp.zeros((page, n_col), F32)
        v_self = jnp.zeros((page, n_col), F32)
        v_far = jnp.zeros((page, n_col), F32)
        for h in range(B_HEADS):
            mine = col // rows == h
            v_last = jnp.where(mine, _bias_from_bucket(bk_last, rb_ref, h), v_last)
            v_self = jnp.where(mine, _bias_from_bucket(bk_self, rb_ref, h), v_self)
            v_far = jnp.where(mine, rb_ref[N_BUCKETS - 1, h], v_far)
        bias_scr[0] = v_last
        bias_scr[1] = jnp.where((d_self >= 0) & (key < t), v_self, NEG_INF)
        bias_scr[2] = v_far

    @pl.when(p == 0)
    def _():
        m_scr[...] = jnp.full(m_scr.shape, NEG_INF, F32)
        l_scr[...] = jnp.zeros(l_scr.shape, F32)
        acc_scr[...] = jnp.zeros(acc_scr.shape, F32)
        w = B_HEADS * LANE
        qrep = jnp.concatenate([q_ref[...]] * (n_col // t), axis=0)
        r = lax.broadcasted_iota(I32, (n_col, w), 0)
        f = lax.broadcasted_iota(I32, (n_col, w), 1)
        keep = (f // LANE == r // rows) & ((f % LANE) // B_DH == (r // t) % 2)
        qbd = jnp.where(keep, qrep, 0.0)
        for g in range(n_pair):
            for hh in range(2):
                c0 = g * pair + hh * LANE
                qbd_scr[g, hh * LANE:(hh + 1) * LANE, :] = qbd[:, c0:c0 + LANE].T.astype(BF16)

    far_row = bias_scr[2, 0:1, :]

    def key_scores(get_pair):
        s = jnp.dot(get_pair(0), qbd_scr[0], preferred_element_type=F32)
        for g in range(1, n_pair):
            s = s + jnp.dot(get_pair(g), qbd_scr[g], preferred_element_type=F32)
        return s

    def online_update(s, values):
        st = jnp.concatenate(s, axis=0)
        m = m_scr[...]
        m_new = jnp.maximum(m, jnp.max(st, axis=0, keepdims=True))
        a = jnp.exp(m - m_new)
        pt_ = jnp.exp(st - m_new)
        m_scr[...] = m_new
        l_scr[...] = l_scr[...] * a + jnp.sum(pt_, axis=0, keepdims=True)
        a_col = jnp.broadcast_to(a, (n_col, n_col)).T
        pq = [pt_[i * page:(i + 1) * page].T.astype(BF16) for i in range(len(s))]
        for h in range(B_HEADS):
            hs = slice(h * rows, (h + 1) * rows)
            pv = jnp.dot(pq[0][hs], values[0](h), preferred_element_type=F32)
            for i in range(1, len(s)):
                pv = pv + jnp.dot(pq[i][hs], values[i](h), preferred_element_type=F32)
            acc_scr[h] = acc_scr[h] * a_col[hs] + pv

    blocks = []
    for i in range(pages_per_step):
        s = key_scores(lambda g, i=i: jnp.concatenate([head_rows(kp[i], 2 * g), head_rows(kp[i], 2 * g + 1)], axis=1))
        if i == pages_per_step - 1:
            blocks.append(s + jnp.where(p == n_steps - 1, bias_scr[0], far_row))
        else:
            blocks.append(s + far_row)
    online_update(blocks, [lambda h, i=i: head_rows(vp[i], h) for i in range(pages_per_step)])

    @pl.when(p == n_steps - 1)
    def _():
        lam, lam_init = _diff_lambda(lam_ref)
        og = og_ref[...]
        kn = jnp.concatenate([kn_ref[...].astype(BF16), jnp.zeros((page - t, B_HEADS * LANE), BF16)], axis=0)
        s = key_scores(lambda g: kn[:, g * pair:(g + 1) * pair]) + bias_scr[1]
        zpad = jnp.zeros((page - t, LANE), BF16)
        online_update([s], [lambda h: jnp.concatenate([vn_ref[:, h * LANE:(h + 1) * LANE].astype(BF16), zpad], axis=0)])
        l_col = jnp.broadcast_to(l_scr[...], (n_col, n_col)).T
        for h in range(B_HEADS):
            hs = slice(h * rows, (h + 1) * rows)
            o_ref[:, h * LANE:(h + 1) * LANE] = _diff_finish((None, l_col[hs], acc_scr[h]), t, lam, lam_init, og)


def _attn_decode(qn, kn, vn, cache_k, cache_v, page_table, rel_bias, b_lambda, og, *, nb, t):
    n_pages = page_table.shape[1]
    n_phys, page = cache_k.shape[1], cache_k.shape[2]
    w = B_HEADS * B_DV
    n_col = B_HEADS * 2 * t
    assert n_col == LANE and page == LANE, "one score lane per (head, map, new token)"
    pps = 8
    while n_pages % pps:
        pps //= 2
    tok = pl.BlockSpec((t, w), lambda b, p, pt: (b, 0))
    ck = cache_k.reshape(n_phys, page * B_HEADS, B_DV)
    cv = cache_v.reshape(n_phys, page * B_HEADS, B_DV)
    pspec = lambda i: pl.BlockSpec((1, page * B_HEADS, B_DV), lambda b, p, pt, i=i: (pt[b, p * pps + i], 0, 0))
    grid_spec = pltpu.PrefetchScalarGridSpec(
        num_scalar_prefetch=1,
        grid=(nb, n_pages // pps),
        in_specs=[
            pl.BlockSpec(memory_space=pltpu.SMEM),
            pl.BlockSpec((4, B_DH), lambda b, p, pt: (0, 0)),
            pl.BlockSpec((1, B_DV), lambda b, p, pt: (0, 0)),
            tok, tok, tok,
        ] + [pspec(i) for i in range(pps)] + [pspec(i) for i in range(pps)],
        out_specs=pl.BlockSpec((t, w), lambda b, p, pt: (b, 0)),
        scratch_shapes=[
            pltpu.VMEM((3, page, n_col), F32),
            pltpu.VMEM((B_HEADS // 2, 2 * LANE, n_col), BF16),
            pltpu.VMEM((1, n_col), F32),
            pltpu.VMEM((1, n_col), F32),
            pltpu.VMEM((B_HEADS, 2 * t, B_DV), F32),
        ],
    )
    return pl.pallas_call(
        functools.partial(_attn_decode_kernel, pages_per_step=pps),
        grid_spec=grid_spec,
        out_shape=jax.ShapeDtypeStruct((nb * t, w), F32),
        compiler_params=_cparams(("arbitrary", "arbitrary")),
        name="attn_decode",
    )(page_table, rel_bias, b_lambda, og.reshape(1, B_DV), qn, kn, vn, *([ck] * pps), *([cv] * pps))


def _merge_kernel(ap_ref, as_ref, bp_ref, bs_ref, za0_ref, za1_ref, zb0_ref, zb1_ref, wa_ref, wb_ref, o_ref, *,
                  prompt_tiles):
    is_prompt = pl.program_id(0) < prompt_tiles
    a = jnp.where(is_prompt, ap_ref[...], as_ref[...]).astype(BF16)
    b = jnp.where(is_prompt, bp_ref[...], bs_ref[...]).astype(BF16)
    ya = jnp.dot(a, wa_ref[...], preferred_element_type=F32)
    yb = jnp.dot(b, wb_ref[...], preferred_element_type=F32)
    half = za0_ref.shape[1]
    for c, (za_ref, zb_ref) in enumerate(((za0_ref, zb0_ref), (za1_ref, zb1_ref))):
        cs = slice(c * half, (c + 1) * half)
        o_ref[:, cs] = (jax.nn.sigmoid(za_ref[...]) * ya[:, cs] + jax.nn.sigmoid(zb_ref[...]) * yb[:, cs]).astype(BF16)


def _merge(a_p, a_s, b_p, b_s, z, wa, wb):
    n_p, n_s = a_p.shape[0], a_s.shape[0]
    wa_in, d = wa.shape
    half = d // 2
    tm = next(t for t in (256, 128) if n_p % t == 0 and n_s % t == 0)
    pt = n_p // tm
    prow = lambda w: pl.BlockSpec((tm, w), lambda i: (jnp.minimum(i, pt - 1), 0))
    srow = lambda w: pl.BlockSpec((tm, w), lambda i: (jnp.maximum(i - pt, 0), 0))
    zcol = lambda off: pl.BlockSpec((tm, half), lambda i, off=off: (i, off // half))
    return pl.pallas_call(
        functools.partial(_merge_kernel, prompt_tiles=pt),
        grid=((n_p + n_s) // tm,),
        in_specs=[
            prow(wa_in), srow(wa_in), prow(wb.shape[0]), srow(wb.shape[0]),
            zcol(OFF_GA), zcol(OFF_GA + half), zcol(OFF_GB), zcol(OFF_GB + half),
            pl.BlockSpec(wa.shape, lambda i: (0, 0)),
            pl.BlockSpec(wb.shape, lambda i: (0, 0)),
        ],
        out_specs=pl.BlockSpec((tm, d), lambda i: (i, 0)),
        out_shape=jax.ShapeDtypeStruct((n_p + n_s, d), BF16),
        compiler_params=_cparams(("arbitrary",)),
        name="gated_merge",
    )(a_p, a_s, b_p, b_s, z, z, z, z, wa, wb)


def _out_proj_kernel(x_ref, m_ref, w_ref, g_ref, x1_ref, h2_ref):
    x1 = x_ref[...] + jnp.dot(m_ref[...], w_ref[...], preferred_element_type=F32)
    x1_ref[...] = x1
    ms = jnp.mean(x1 * x1, axis=-1, keepdims=True)
    h2_ref[...] = (x1 * lax.rsqrt(ms + EPS) * g_ref[...]).astype(BF16)


def _out_proj(x, merged, w_out, g2):
    n, d = x.shape
    tm = _row_tile(n, (256, 128))
    row = pl.BlockSpec((tm, d), lambda i: (i, 0))
    return pl.pallas_call(
        _out_proj_kernel,
        grid=(n // tm,),
        in_specs=[row, row, pl.BlockSpec((d, d), lambda i: (0, 0)), pl.BlockSpec((1, d), lambda i: (0, 0))],
        out_specs=[row, row],
        out_shape=[jax.ShapeDtypeStruct((n, d), F32), jax.ShapeDtypeStruct((n, d), BF16)],
        compiler_params=_cparams(("arbitrary",)),
        name="out_proj_norm2",
    )(x, merged, w_out, g2.reshape(1, d))


def _peer_scores_kernel(h_ref, wq_ref, k1_ref, k2_ref, s_ref):
    q = jnp.dot(h_ref[...], wq_ref[...], preferred_element_type=F32).astype(BF16)
    half = q.shape[1] // (2 * PEER_HEADS)
    for h in range(PEER_HEADS):
        for c, kr in enumerate((k1_ref, k2_ref)):
            qs = q[:, (2 * h + c) * half:(2 * h + c + 1) * half]
            s_ref[2 * h + c] = lax.dot_general(kr[h], qs, _NT, preferred_element_type=F32)


def _peer_scores(h2, wq, k1, k2):
    n, d = h2.shape
    tm = _row_tile(n, (256, 128))
    kspec = pl.BlockSpec(k1.shape, lambda i: (0, 0, 0))
    return pl.pallas_call(
        _peer_scores_kernel,
        grid=(n // tm,),
        in_specs=[pl.BlockSpec((tm, d), lambda i: (i, 0)), pl.BlockSpec(wq.shape, lambda i: (0, 0)), kspec, kspec],
        out_specs=pl.BlockSpec((2 * PEER_HEADS, PEER_NKEYS, tm), lambda i: (0, 0, i)),
        out_shape=jax.ShapeDtypeStruct((2 * PEER_HEADS, PEER_NKEYS, n), F32),
        compiler_params=_cparams(("arbitrary",)),
        name="peer_scores",
    )(h2, wq, k1, k2)


def _cand_layout():
    k = PEER_TOPK
    groups = [(0, k)] + [(j, 8) for j in range(1, 8)]
    pos = []
    for j, rows in groups:
        for l in range(rows):
            pos.append(j * k + l if (j + 1) * (l + 1) <= k else 1 << 20)
    pos += [j * k for j in range(8, k)]
    return groups, pos


def _topk_rows(s, k):
    n_rows = s.shape[0]
    row = lax.broadcasted_iota(I32, s.shape, 0)
    vals, idxs = [], []
    for _ in range(k):
        m = jnp.max(s, axis=0, keepdims=True)
        idx = jnp.min(jnp.where(s == m, row, n_rows), axis=0, keepdims=True)
        vals.append(m)
        idxs.append(idx)
        s = jnp.where(row == idx, NEG_INF, s)
    return jnp.concatenate(vals, axis=0), jnp.concatenate(idxs, axis=0)


def _peer_topk_kernel(s_ref, pos_ref, i1_ref, i2_ref, g_ref):
    k = PEER_TOPK
    groups, _ = _cand_layout()
    pos = pos_ref[...]
    big = 1 << 20
    e_rows, g_rows = [], []
    for h in range(PEER_HEADS):
        v1, i1 = _topk_rows(s_ref[2 * h], k)
        v2, i2 = _topk_rows(s_ref[2 * h + 1], k)
        cand, cidx = [], []
        for j, rows in groups:
            cand.append(v1[j:j + 1] + v2[0:rows])
            cidx.append(i1[j:j + 1] * PEER_NKEYS + i2[0:rows])
        cand.append(v1[8:k] + v2[0:1])
        cidx.append(i1[8:k] * PEER_NKEYS + i2[0:1])
        cand = jnp.where(pos < big, jnp.concatenate(cand, axis=0), NEG_INF)
        cidx = jnp.concatenate(cidx, axis=0)
        sc, ex = [], []
        for _ in range(k):
            m = jnp.max(cand, axis=0, keepdims=True)
            pidx = jnp.min(jnp.where(cand == m, pos, big), axis=0, keepdims=True)
            sel = pos == pidx
            sc.append(m)
            ex.append(jnp.max(jnp.where(sel, cidx, -1), axis=0, keepdims=True))
            cand = jnp.where(sel, NEG_INF, cand)
        sc = jnp.concatenate(sc, axis=0)
        ew = jnp.exp(sc - sc[0:1])
        g_rows.append(ew / jnp.sum(ew, axis=0, keepdims=True))
        e_rows.append(jnp.concatenate(ex, axis=0))
    e_all = jnp.concatenate(e_rows, axis=0)
    g_all = jnp.concatenate(g_rows, axis=0)
    i1_ref[...] = (e_all >> 7).astype(F32).T
    i2_ref[...] = (e_all & (PEER_NKEYS - 1)).astype(F32).T
    g_ref[...] = g_all.T


def _peer_topk(scores_t):
    n = scores_t.shape[2]
    tl = LANE
    _, pos = _cand_layout()
    pos_tab = jnp.broadcast_to(jnp.asarray(pos, I32)[:, None], (len(pos), tl))
    nsel = PEER_HEADS * PEER_TOPK
    ospec = pl.BlockSpec((tl, nsel), lambda i: (i, 0))
    oshape = jax.ShapeDtypeStruct((n, nsel), F32)
    return pl.pallas_call(
        _peer_topk_kernel,
        grid=(n // tl,),
        in_specs=[pl.BlockSpec((2 * PEER_HEADS, PEER_NKEYS, tl), lambda i: (0, 0, i)),
                  pl.BlockSpec(pos_tab.shape, lambda i: (0, 0))],
        out_specs=[ospec, ospec, ospec],
        out_shape=[oshape, oshape, oshape],
        compiler_params=_cparams(("arbitrary",)),
        name="peer_topk",
    )(scores_t, pos_tab)


def _peer_gates_kernel(i1_ref, i2_ref, g_ref, o_ref, x_scr):
    tb = i1_ref.shape[0]
    grp = 16
    sub = lax.broadcasted_iota(I32, (PEER_NKEYS, LANE), 0).astype(F32)
    ro = lax.broadcasted_iota(I32, (grp * grp, grp * grp), 0)
    ci = lax.broadcasted_iota(I32, (grp * grp, grp * grp), 1)
    swap = jnp.where((ro % grp) * grp + ro // grp == ci, 1.0, 0.0).astype(BF16)

    def group(gi, carry):
        r0 = pl.multiple_of(gi * grp, grp)
        for t in range(grp):
            i1 = i1_ref[pl.ds(r0 + t, 1), :]
            i2 = i2_ref[pl.ds(r0 + t, 1), :]
            g = g_ref[pl.ds(r0 + t, 1), :]
            p1 = jnp.where(i1 == sub, g, 0.0).astype(BF16)
            p2 = jnp.where(i2 == sub, 1.0, 0.0).astype(BF16)
            x_scr[t * PEER_NKEYS:(t + 1) * PEER_NKEYS, :] = lax.dot_general(
                p1, p2, _NT, preferred_element_type=F32).astype(BF16)
        for kb in range(PEER_NKEYS // grp):
            rows = [x_scr[t * PEER_NKEYS + kb * grp:t * PEER_NKEYS + (kb + 1) * grp, :] for t in range(grp)]
            w = jnp.dot(swap, jnp.concatenate(rows, axis=0), preferred_element_type=F32)
            for j in range(grp):
                o_ref[kb * grp + j, pl.ds(r0, grp), :] = w[j * grp:(j + 1) * grp].astype(BF16)
        return carry

    lax.fori_loop(0, tb // grp, group, 0)


def _peer_gates(i1, i2, g):
    n, nsel = i1.shape
    tb = LANE
    ispec = pl.BlockSpec((tb, nsel), lambda i: (i, 0))
    return pl.pallas_call(
        _peer_gates_kernel,
        grid=(n // tb,),
        in_specs=[ispec, ispec, ispec],
        out_specs=pl.BlockSpec((PEER_NKEYS, tb, PEER_NKEYS), lambda i: (0, i, 0)),
        out_shape=jax.ShapeDtypeStruct((PEER_NKEYS, n, PEER_NKEYS), BF16),
        scratch_shapes=[pltpu.VMEM((16 * PEER_NKEYS, PEER_NKEYS), BF16)],
        compiler_params=_cparams(("arbitrary",)),
        name="peer_gates",
    )(i1, i2, g)


def _peer_dense_kernel(h_ref, x1_ref, u_ref, v_ref, g_ref, o_ref):
    @pl.when(pl.program_id(1) == 0)
    def _():
        o_ref[...] = x1_ref[...]

    a = lax.dot_general(h_ref[...], u_ref[...], _NT, preferred_element_type=F32)
    ws = []
    for c in range(g_ref.shape[0]):
        ac = a[:, c * LANE:(c + 1) * LANE]
        gelu = 0.5 * ac * (1.0 + lax.erf(ac * SQRT_HALF))
        ws.append((gelu * g_ref[c].astype(F32)).astype(BF16))
    w = jnp.concatenate(ws, axis=1)
    o_ref[...] += jnp.dot(w, v_ref[...], preferred_element_type=F32)


def _peer_dense(h2, x1, u, v, gates):
    n, d = h2.shape
    n_exp = u.shape[0]
    tm = _row_tile(n)
    te = 512
    ge = te // PEER_NKEYS
    row = lambda i, e: (i, 0)
    return pl.pallas_call(
        _peer_dense_kernel,
        grid=(n // tm, n_exp // te),
        in_specs=[
            pl.BlockSpec((tm, d), row),
            pl.BlockSpec((tm, d), row),
            pl.BlockSpec((te, d), lambda i, e: (e, 0)),
            pl.BlockSpec((te, d), lambda i, e: (e, 0)),
            pl.BlockSpec((ge, tm, PEER_NKEYS), lambda i, e: (e, i, 0)),
        ],
        out_specs=pl.BlockSpec((tm, d), row),
        out_shape=jax.ShapeDtypeStruct((n, d), F32),
        compiler_params=_cparams(("arbitrary", "arbitrary")),
        name="peer_dense",
    )(h2, x1, u, v, gates)


def kernel(x_prompt, x_sample, cache_k, cache_v, state_hgrn, page_table, meta_tokens, lb_logits, rel_bias,
           norm1_g, w_in, a_onorm_g, a_proj, b_qnorm_g, b_knorm_g, b_lambda, b_onorm_g, b_proj, w_out,
           norm2_g, peer_wq, peer_k1, peer_k2, peer_u, peer_v):
    assert w_in.shape[0] == 1, "one layer"
    nb, seq, d = x_prompt.shape
    nbs, t = x_sample.shape[:2]
    lp = seq + LANE
    n_p, n_s = nb * lp, nbs * t
    n = n_p + n_s

    front = jnp.concatenate([jnp.zeros((PAD_FRONT, d), F32), meta_tokens.astype(F32)], axis=0)
    pieces = []
    for b in range(nb):
        pieces += [front, x_prompt[b]]
    x = jnp.concatenate(pieces + [x_sample.reshape(n_s, d)], axis=0)

    bf = lambda w: w.astype(BF16)
    z = _rms_matmul(x, norm1_g[0], bf(w_in[0]), 1024)

    a_p, s_p = _hgrn(z, lb_logits, a_onorm_g[0], jnp.zeros((nb, A_HEADS, A_DK, A_DV), F32),
                     row0=0, nb=nb, t_len=lp, c=HGRN_CHUNK, hp=HGRN_HEADS_PER_STEP)
    a_s, s_s = _hgrn(z, lb_logits, a_onorm_g[0], state_hgrn[0], row0=n_p, nb=nbs, t_len=t, c=t, hp=A_HEADS)

    qn, qt, kn, knb, vt = _qk_norm(z, b_qnorm_g[0], b_knorm_g[0])
    b_p = _attn_prompt(qt, knb, vt, rel_bias, b_lambda[0], b_onorm_g[0], nb=nb, lp=lp)
    w = B_HEADS * B_DV
    b_s = _attn_decode(qn[n_p:].astype(F32), kn[n_p:], z[n_p:, OFF_BV:OFF_BV + w], cache_k, cache_v, page_table,
                       rel_bias, b_lambda[0], b_onorm_g[0], nb=nbs, t=t)

    merged = _merge(a_p, a_s, b_p, b_s, z, bf(a_proj[0]), bf(b_proj[0]))
    x1, h2 = _out_proj(x, merged, bf(w_out[0]), norm2_g[0])

    scores_t = _peer_scores(h2, bf(peer_wq[0]), bf(peer_k1[0]), bf(peer_k2[0]))
    i1, i2, g = _peer_topk(scores_t)
    gates = _peer_gates(i1, i2, g)
    y = _peer_dense(h2, x1, bf(peer_u[0]), bf(peer_v[0]), gates)

    y_prompt = y[:n_p].reshape(nb, lp, d)[:, LANE:]
    y_sample = y[n_p:].reshape(nbs, t, d)
    k_prompt = kn[:n_p].reshape(nb, lp, B_HEADS, 2 * B_DH)[:, PAD_FRONT:][None]
    v_prompt = z[:n_p, OFF_BV:OFF_BV + w].reshape(nb, lp, B_HEADS, B_DV)[:, PAD_FRONT:][None]
    k_sample = kn[n_p:].reshape(nbs, t, B_HEADS, 2 * B_DH)[None]
    v_sample = z[n_p:, OFF_BV:OFF_BV + w].reshape(nbs, t, B_HEADS, B_DV)[None]
    return (y_prompt, y_sample, k_prompt, v_prompt, s_p[None], k_sample, v_sample, s_s[None])
```

```python
import functools
import math

import jax
import jax.numpy as jnp
from jax import lax
from jax.experimental import pallas as pl
from jax.experimental.pallas import tpu as pltpu

F32 = jnp.float32
BF16 = jnp.bfloat16
I32 = jnp.int32

LANE = 128
D_MODEL = 2048
N_META = 16
PAD_FRONT = LANE - N_META
A_HEADS = 8
A_DK = 128
A_DV = 128
B_HEADS = 8
B_DH = 64
B_DV = 128
N_BUCKETS = 32
MAX_DISTANCE = 128
PEER_HEADS = 8
PEER_NKEYS = 128
PEER_TOPK = 16
EPS = 1e-6
NEG_INF = -1e30
ATTN_SCALE = B_DH ** -0.5
SQRT_HALF = 0.7071067811865476
HGRN_CHUNK = 16
HGRN_HEADS_PER_STEP = 4
VMEM_LIMIT = 56 * 1024 * 1024

OFF_AQ, OFF_AF, OFF_AI, OFF_AG = 0, 1024, 2048, 3072
OFF_BQ, OFF_BK, OFF_BV = 4096, 5120, 6144
OFF_GA, OFF_GB = 7168, 9216
IN_TOTAL = 11264

_NT = (((1,), (1,)), ((), ()))
_TN = (((0,), (0,)), ((), ()))


def _cparams(sem):
    return pltpu.CompilerParams(dimension_semantics=sem, vmem_limit_bytes=VMEM_LIMIT)


def _row_tile(n, prefs=(768, 512, 384, 256, 128)):
    for t in prefs:
        if n % t == 0:
            return t
    raise ValueError(f"token count {n} is not a multiple of 128")


def _rms_matmul_kernel(x_ref, g_ref, w_ref, o_ref, h_ref):
    @pl.when(pl.program_id(1) == 0)
    def _():
        x = x_ref[...]
        ms = jnp.mean(x * x, axis=-1, keepdims=True)
        h_ref[...] = (x * lax.rsqrt(ms + EPS) * g_ref[...]).astype(BF16)

    o_ref[...] = jnp.dot(h_ref[...], w_ref[...], preferred_element_type=F32)


def _rms_matmul(x, g, w_bf16, tn):
    n, d = x.shape
    nc = w_bf16.shape[1]
    tm = _row_tile(n)
    return pl.pallas_call(
        _rms_matmul_kernel,
        grid=(n // tm, nc // tn),
        in_specs=[
            pl.BlockSpec((tm, d), lambda i, j: (i, 0)),
            pl.BlockSpec((1, d), lambda i, j: (0, 0)),
            pl.BlockSpec((d, tn), lambda i, j: (0, j)),
        ],
        out_specs=pl.BlockSpec((tm, tn), lambda i, j: (i, j)),
        out_shape=jax.ShapeDtypeStruct((n, nc), F32),
        scratch_shapes=[pltpu.VMEM((tm, d), BF16)],
        compiler_params=_cparams(("arbitrary", "arbitrary")),
        name="rms_in_proj",
    )(x, g.reshape(1, d), w_bf16)


def _hgrn_kernel(lbl_ref, gn_ref, q_ref, f_ref, i_ref, g_ref, s0_ref, o_ref, sfin_ref, st_scr, *, c, n_chunks, hp):
    for hh in range(hp):
        st_scr[hh] = s0_ref[0, hh].T
    lg = lbl_ref[...]
    e = jnp.exp(lg - jnp.max(lg, axis=0, keepdims=True))
    lb_all = e[0:1] / jnp.sum(e, axis=0, keepdims=True)
    gn = gn_ref[...]
    row = lax.broadcasted_iota(I32, (c, A_DK), 0)
    ones = jnp.ones((A_DK, A_DV), BF16)

    def chunk(ci, carry):
        r0 = pl.multiple_of(ci * c, c)
        for hh in range(hp):
            head_chunk(r0, hh)
        return carry

    def head_chunk(r0, hh):
        sl = slice(hh * A_DK, (hh + 1) * A_DK)
        lb = lb_all[:, sl]
        q = q_ref[pl.ds(r0, c), sl]
        zf = f_ref[pl.ds(r0, c), sl]
        v = i_ref[pl.ds(r0, c), sl]
        zg = g_ref[pl.ds(r0, c), sl]
        f = lb + (1.0 - lb) * jax.nn.sigmoid(zf)
        k = 1.0 - f
        b = jnp.log(f)
        sh = 1
        while sh < c:
            b = b + jnp.where(row >= sh, pltpu.roll(b, sh, 0), 0.0)
            sh *= 2
        sub = 8
        ps = []
        for s in range(c):
            t0 = s // sub * sub
            d = jnp.exp(jnp.where(row[t0:] >= s, b[t0:] - b[s:s + 1, :], NEG_INF))
            ps.append(q[t0:] * d * k[s:s + 1, :])
        p = jnp.concatenate(ps, axis=0).astype(BF16)
        a = jnp.dot(p, ones, preferred_element_type=F32)
        o_groups = [None] * (c // sub)
        off = 0
        for s in range(c):
            t0 = s // sub * sub
            for gi in range(t0 // sub, c // sub):
                r0_ = off + gi * sub - t0
                piece = a[r0_:r0_ + sub] * v[s:s + 1, :]
                o_groups[gi] = piece if o_groups[gi] is None else o_groups[gi] + piece
            off += c - t0
        o = jnp.concatenate(o_groups, axis=0)
        st = st_scr[hh]
        qe = (q * jnp.exp(b)).astype(BF16)
        o = o + lax.dot_general(qe, st.astype(BF16), _NT, preferred_element_type=F32)
        bc = b[c - 1:c, :]
        kt = (k * jnp.exp(bc - b)).astype(BF16)
        upd = lax.dot_general(v.astype(BF16), kt, _TN, preferred_element_type=F32)
        st_scr[hh] = st * jnp.exp(bc) + upd
        ms = jnp.mean(o * o, axis=-1, keepdims=True)
        o_ref[pl.ds(r0, c), sl] = o * lax.rsqrt(ms + EPS) * gn * (zg * jax.nn.sigmoid(zg))

    lax.fori_loop(0, n_chunks, chunk, 0, unroll=4 if n_chunks % 4 == 0 else 1)
    for hh in range(hp):
        sfin_ref[0, hh] = st_scr[hh].T


def _hgrn(z, lb_logits, gn, s0, *, row0, nb, t_len, c, hp):
    assert row0 % t_len == 0 and t_len % c == 0 and A_HEADS % hp == 0
    rb0 = row0 // t_len
    wb = hp * A_DK
    zspec = lambda off: pl.BlockSpec((t_len, wb), lambda b, h, off=off: (rb0 + b, off // wb + h))
    return pl.pallas_call(
        functools.partial(_hgrn_kernel, c=c, n_chunks=t_len // c, hp=hp),
        grid=(nb, A_HEADS // hp),
        in_specs=[
            pl.BlockSpec((lb_logits.shape[0], wb), lambda b, h: (0, h)),
            pl.BlockSpec((1, A_DV), lambda b, h: (0, 0)),
            zspec(OFF_AQ), zspec(OFF_AF), zspec(OFF_AI), zspec(OFF_AG),
            pl.BlockSpec((1, hp, A_DK, A_DV), lambda b, h: (b, h, 0, 0)),
        ],
        out_specs=[
            pl.BlockSpec((t_len, wb), lambda b, h: (b, h)),
            pl.BlockSpec((1, hp, A_DK, A_DV), lambda b, h: (b, h, 0, 0)),
        ],
        out_shape=[
            jax.ShapeDtypeStruct((nb * t_len, A_HEADS * A_DV), F32),
            jax.ShapeDtypeStruct((nb, A_HEADS, A_DK, A_DV), F32),
        ],
        scratch_shapes=[pltpu.VMEM((hp, A_DV, A_DK), F32)],
        compiler_params=_cparams(("arbitrary", "arbitrary")),
        name=f"hgrn_c{c}",
    )(lb_logits, gn.reshape(1, A_DV), z, z, z, z, s0)


def _qk_norm_kernel(zq_ref, zk_ref, zv_ref, qg_ref, kg_ref, qn_ref, qt_ref, kn_ref, knb_ref, vt_ref):
    r = lax.broadcasted_iota(I32, (LANE, LANE), 0) // B_DH
    cc = lax.broadcasted_iota(I32, (LANE, LANE), 1) // B_DH
    seg = jnp.where(r == cc, 1.0, 0.0).astype(BF16)

    def norm(z, g):
        sq = z * z
        hi = sq.astype(BF16)
        lo = (sq - hi.astype(F32)).astype(BF16)
        ss = jnp.dot(hi, seg, preferred_element_type=F32) + jnp.dot(lo, seg, preferred_element_type=F32)
        return z * lax.rsqrt(ss * (1.0 / B_DH) + EPS) * g

    qg = qg_ref[...]
    kg = kg_ref[...]
    for h in range(B_HEADS):
        sl = slice(h * LANE, (h + 1) * LANE)
        qn = norm(zq_ref[:, sl], qg) * ATTN_SCALE
        qn_ref[:, sl] = qn.astype(BF16)
        qt_ref[sl, :] = qn.T.astype(BF16)
        kn = norm(zk_ref[:, sl], kg)
        kn_ref[:, sl] = kn
        knb_ref[:, sl] = kn.astype(BF16)
        vt_ref[sl, :] = zv_ref[:, sl].T.astype(BF16)


def _qk_norm(z, qg, kg):
    n = z.shape[0]
    tm = _row_tile(n)
    w = B_HEADS * 2 * B_DH
    zspec = lambda off: pl.BlockSpec((tm, w), lambda i, off=off: (i, off // w))
    ospec = pl.BlockSpec((tm, w), lambda i: (i, 0))
    tspec = pl.BlockSpec((w, tm), lambda i: (0, i))
    gspec = pl.BlockSpec((1, LANE), lambda i: (0, 0))
    return pl.pallas_call(
        _qk_norm_kernel,
        grid=(n // tm,),
        in_specs=[zspec(OFF_BQ), zspec(OFF_BK), zspec(OFF_BV), gspec, gspec],
        out_specs=[ospec, tspec, ospec, ospec, tspec],
        out_shape=[
            jax.ShapeDtypeStruct((n, w), BF16),
            jax.ShapeDtypeStruct((w, n), BF16),
            jax.ShapeDtypeStruct((n, w), F32),
            jax.ShapeDtypeStruct((n, w), BF16),
            jax.ShapeDtypeStruct((w, n), BF16),
        ],
        compiler_params=_cparams(("arbitrary",)),
        name="qk_norm",
    )(z, z, z, jnp.tile(qg, 2).reshape(1, LANE), jnp.tile(kg, 2).reshape(1, LANE))


def _rel_bucket(dist):
    n = jnp.maximum(dist, 0)
    max_exact = N_BUCKETS // 2
    nf = jnp.maximum(n, 1).astype(F32)
    large = max_exact + (jnp.log(nf / max_exact) / math.log(MAX_DISTANCE / max_exact)
                         * (N_BUCKETS - max_exact)).astype(I32)
    large = jnp.minimum(large, N_BUCKETS - 1)
    return jnp.where(n < max_exact, n, large)


def _bias_from_bucket(bucket, rb_ref, h):
    val = jnp.zeros(bucket.shape, F32)
    for bk in range(N_BUCKETS):
        val = jnp.where(bucket == bk, rb_ref[bk, h], val)
    return val


def _diff_lambda(lam_ref):
    lp = lam_ref[...]
    lam_init = 0.8 - 0.6 * math.exp(-0.3 * 0)
    lam = (jnp.exp(jnp.sum(lp[0:1] * lp[1:2], axis=-1, keepdims=True))
           - jnp.exp(jnp.sum(lp[2:3] * lp[3:4], axis=-1, keepdims=True)) + lam_init)
    return lam, lam_init


def _split_maps(qh):
    lane = lax.broadcasted_iota(I32, qh.shape, 1)
    zero = jnp.zeros_like(qh)
    return jnp.concatenate([jnp.where(lane < B_DH, qh, zero), jnp.where(lane >= B_DH, qh, zero)], axis=0)


def _online_update(carry, s, vj):
    m, l, acc = carry
    m_new = jnp.maximum(m, jnp.max(s, axis=-1, keepdims=True))
    a = jnp.exp(m - m_new)
    p = jnp.exp(s - m_new)
    l = l * a + jnp.sum(p, axis=-1, keepdims=True)
    acc = acc * a + jnp.dot(p.astype(BF16), vj, preferred_element_type=F32)
    return m_new, l, acc


def _diff_finish(carry, t, lam, lam_init, og):
    _, l, acc = carry
    o = acc[0:t] / l[0:t] - lam * (acc[t:2 * t] / l[t:2 * t])
    ms = jnp.mean(o * o, axis=-1, keepdims=True)
    return o * lax.rsqrt(ms + EPS) * og * (1.0 - lam_init)


_BIAS_DIAG, _BIAS_NEAR, _BIAS_DIAG_PAD, _BIAS_NEAR_PAD, _BIAS_FAR_PAD = range(5)


def _attn_prompt_kernel(rb_ref, lam_ref, og_ref, qt_ref, k_ref, vt_ref, o_ref, bias_scr, q2_scr, m_scr, l_scr, acc_scr,
                        s_scr):
    i = pl.program_id(1)
    tq = qt_ref.shape[1]

    @pl.when((pl.program_id(0) == 0) & (i == 0))
    def _():
        r = lax.broadcasted_iota(I32, (tq, 2 * tq), 0)
        qi = lax.broadcasted_iota(I32, (tq, 2 * tq), 1) % tq
        in_pad = r < PAD_FRONT
        for blk in range(2):
            d = qi - r + blk * tq
            bucket = _rel_bucket(d)
            for h in range(B_HEADS):
                val = _bias_from_bucket(bucket, rb_ref, h)
                if blk == 0:
                    val = jnp.where(d >= 0, val, NEG_INF)
                bias_scr[_BIAS_DIAG + blk, h] = val
                bias_scr[_BIAS_DIAG_PAD + blk, h] = jnp.where(in_pad, NEG_INF, val)
        for h in range(B_HEADS):
            bias_scr[_BIAS_FAR_PAD, h] = jnp.where(in_pad, NEG_INF, rb_ref[N_BUCKETS - 1, h])

    row = lax.broadcasted_iota(I32, (LANE, tq), 0)
    for h in range(B_HEADS):
        qt = qt_ref[h * LANE:(h + 1) * LANE, :]
        zero = jnp.zeros_like(qt)
        q2_scr[h] = jnp.concatenate([jnp.where(row < B_DH, qt, zero), jnp.where(row >= B_DH, qt, zero)], axis=1)
    m_scr[...] = jnp.full(m_scr.shape, NEG_INF, F32)
    l_scr[...] = jnp.zeros(l_scr.shape, F32)
    acc_scr[...] = jnp.zeros(acc_scr.shape, F32)

    def step(j, bias_of):
        k0 = pl.multiple_of(j * tq, tq)
        for h in range(B_HEADS):
            sl = slice(h * LANE, (h + 1) * LANE)
            s_scr[h] = jnp.dot(k_ref[pl.ds(k0, tq), sl], q2_scr[h], preferred_element_type=F32)
        for h in range(B_HEADS):
            sl = slice(h * LANE, (h + 1) * LANE)
            s = s_scr[h] + bias_of(h)
            m = m_scr[h]
            m_new = jnp.maximum(m, jnp.max(s, axis=0, keepdims=True))
            a = jnp.exp(m - m_new)
            p = jnp.exp(s - m_new)
            m_scr[h] = m_new
            l_scr[h] = l_scr[h] * a + jnp.sum(p, axis=0, keepdims=True)
            acc_scr[h] = acc_scr[h] * a + jnp.dot(vt_ref[sl, pl.ds(k0, tq)], p.astype(BF16),
                                                  preferred_element_type=F32)

    first = jnp.where(i == 0, _BIAS_DIAG_PAD, jnp.where(i == 1, _BIAS_NEAR_PAD, _BIAS_FAR_PAD))
    step(0, lambda h: bias_scr[first, h])

    def far_body(j, carry):
        step(j, lambda h: rb_ref[N_BUCKETS - 1, h])
        return carry

    lax.fori_loop(1, jnp.maximum(i - 1, 1), far_body, 0)

    @pl.when(i >= 2)
    def _():
        step(i - 1, lambda h: bias_scr[_BIAS_NEAR, h])

    @pl.when(i >= 1)
    def _():
        step(i, lambda h: bias_scr[_BIAS_DIAG, h])

    lam, lam_init = _diff_lambda(lam_ref)
    og = og_ref[...]
    for h in range(B_HEADS):
        acc = acc_scr[h]
        l = l_scr[h]
        o = (acc[:, :tq] / l[:, :tq] - lam * (acc[:, tq:] / l[:, tq:])).T
        ms = jnp.mean(o * o, axis=-1, keepdims=True)
        o_ref[:, h * LANE:(h + 1) * LANE] = o * lax.rsqrt(ms + EPS) * og * (1.0 - lam_init)


def _attn_prompt(qt, knb, vt, rel_bias, b_lambda, og, *, nb, lp):
    tq = LANE
    nq = lp // tq
    w = B_HEADS * B_DV
    return pl.pallas_call(
        _attn_prompt_kernel,
        grid=(nb, nq),
        in_specs=[
            pl.BlockSpec(memory_space=pltpu.SMEM),
            pl.BlockSpec((4, B_DH), lambda b, i: (0, 0)),
            pl.BlockSpec((1, B_DV), lambda b, i: (0, 0)),
            pl.BlockSpec((w, tq), lambda b, i: (0, b * nq + i)),
            pl.BlockSpec((lp, w), lambda b, i: (b, 0)),
            pl.BlockSpec((w, lp), lambda b, i: (0, b)),
        ],
        out_specs=pl.BlockSpec((tq, w), lambda b, i: (b * nq + i, 0)),
        out_shape=jax.ShapeDtypeStruct((nb * lp, w), F32),
        scratch_shapes=[
            pltpu.VMEM((5, B_HEADS, tq, 2 * tq), F32),
            pltpu.VMEM((B_HEADS, LANE, 2 * tq), BF16),
            pltpu.VMEM((B_HEADS, 1, 2 * tq), F32),
            pltpu.VMEM((B_HEADS, 1, 2 * tq), F32),
            pltpu.VMEM((B_HEADS, B_DV, 2 * tq), F32),
            pltpu.VMEM((B_HEADS, tq, 2 * tq), F32),
        ],
        compiler_params=_cparams(("arbitrary", "arbitrary")),
        name="attn_prompt",
    )(rel_bias, b_lambda, og.reshape(1, B_DV), qt, knb, vt)


def _attn_decode_kernel(pt_ref, rb_ref, lam_ref, og_ref, q_ref, kn_ref, vn_ref, *rest, pages_per_step):
    kp = rest[:pages_per_step]
    vp = rest[pages_per_step:2 * pages_per_step]
    o_ref, bias_scr, qbd_scr, m_scr, l_scr, acc_scr = rest[2 * pages_per_step:]
    del pt_ref
    b = pl.program_id(0)
    p = pl.program_id(1)
    n_steps = pl.num_programs(1)
    t = q_ref.shape[0]
    page = kp[0].shape[1] // B_HEADS
    rows = 2 * t
    n_col = B_HEADS * rows
    pair = 2 * LANE
    n_pair = B_HEADS // 2

    def head_rows(ref, h):
        return ref[0, pl.ds(h, page, stride=B_HEADS), :].astype(BF16)

    @pl.when((b == 0) & (p == 0))
    def _():
        key = lax.broadcasted_iota(I32, (page, n_col), 0)
        col = lax.broadcasted_iota(I32, (page, n_col), 1)
        tok = col % t
        bk_last = _rel_bucket(page - key + tok)
        d_self = tok - key
        bk_self = _rel_bucket(d_self)
        v_last = jnp.zeros((page, n_col), F32)
        v_self = jnp.zeros((page, n_col), F32)
        v_far = jnp.zeros((page, n_col), F32)
        for h in range(B_HEADS):
            mine = col // rows == h
            v_last = jnp.where(mine, _bias_from_bucket(bk_last, rb_ref, h), v_last)
            v_self = jnp.where(mine, _bias_from_bucket(bk_self, rb_ref, h), v_self)
            v_far = jnp.where(mine, rb_ref[N_BUCKETS - 1, h], v_far)
        bias_scr[0] = v_last
        bias_scr[1] = jnp.where((d_self >= 0) & (key < t), v_self, NEG_INF)
        bias_scr[2] = v_far

    @pl.when(p == 0)
    def _():
        m_scr[...] = jnp.full(m_scr.shape, NEG_INF, F32)
        l_scr[...] = jnp.zeros(l_scr.shape, F32)
        acc_scr[...] = jnp.zeros(acc_scr.shape, F32)
        w = B_HEADS * LANE
        qrep = jnp.concatenate([q_ref[...]] * (n_col // t), axis=0)
        r = lax.broadcasted_iota(I32, (n_col, w), 0)
        f = lax.broadcasted_iota(I32, (n_col, w), 1)
        keep = (f // LANE == r // rows) & ((f % LANE) // B_DH == (r // t) % 2)
        qbd = jnp.where(keep, qrep, 0.0)
        for g in range(n_pair):
            for hh in range(2):
                c0 = g * pair + hh * LANE
                qbd_scr[g, hh * LANE:(hh + 1) * LANE, :] = qbd[:, c0:c0 + LANE].T.astype(BF16)

    far_row = bias_scr[2, 0:1, :]

    def key_scores(get_pair):
        s = jnp.dot(get_pair(0), qbd_scr[0], preferred_element_type=F32)
        for g in range(1, n_pair):
            s = s + jnp.dot(get_pair(g), qbd_scr[g], preferred_element_type=F32)
        return s

    def online_update(s, values):
        st = jnp.concatenate(s, axis=0)
        m = m_scr[...]
        m_new = jnp.maximum(m, jnp.max(st, axis=0, keepdims=True))
        a = jnp.exp(m - m_new)
        pt_ = jnp.exp(st - m_new)
        m_scr[...] = m_new
        l_scr[...] = l_scr[...] * a + jnp.sum(pt_, axis=0, keepdims=True)
        a_col = jnp.broadcast_to(a, (n_col, n_col)).T
        pq = [pt_[i * page:(i + 1) * page].T.astype(BF16) for i in range(len(s))]
        for h in range(B_HEADS):
            hs = slice(h * rows, (h + 1) * rows)
            pv = jnp.dot(pq[0][hs], values[0](h), preferred_element_type=F32)
            for i in range(1, len(s)):
                pv = pv + jnp.dot(pq[i][hs], values[i](h), preferred_element_type=F32)
            acc_scr[h] = acc_scr[h] * a_col[hs] + pv

    blocks = []
    for i in range(pages_per_step):
        s = key_scores(lambda g, i=i: jnp.concatenate([head_rows(kp[i], 2 * g), head_rows(kp[i], 2 * g + 1)], axis=1))
        if i == pages_per_step - 1:
            blocks.append(s + jnp.where(p == n_steps - 1, bias_scr[0], far_row))
        else:
            blocks.append(s + far_row)
    online_update(blocks, [lambda h, i=i: head_rows(vp[i], h) for i in range(pages_per_step)])

    @pl.when(p == n_steps - 1)
    def _():
        lam, lam_init = _diff_lambda(lam_ref)
        og = og_ref[...]
        kn = jnp.concatenate([kn_ref[...].astype(BF16), jnp.zeros((page - t, B_HEADS * LANE), BF16)], axis=0)
        s = key_scores(lambda g: kn[:, g * pair:(g + 1) * pair]) + bias_scr[1]
        zpad = jnp.zeros((page - t, LANE), BF16)
        online_update([s], [lambda h: jnp.concatenate([vn_ref[:, h * LANE:(h + 1) * LANE].astype(BF16), zpad], axis=0)])
        l_col = jnp.broadcast_to(l_scr[...], (n_col, n_col)).T
        for h in range(B_HEADS):
            hs = slice(h * rows, (h + 1) * rows)
            o_ref[:, h * LANE:(h + 1) * LANE] = _diff_finish((None, l_col[hs], acc_scr[h]), t, lam, lam_init, og)


def _attn_decode(qn, kn, vn, cache_k, cache_v, page_table, rel_bias, b_lambda, og, *, nb, t):
    n_pages = page_table.shape[1]
    n_phys, page = cache_k.shape[1], cache_k.shape[2]
    w = B_HEADS * B_DV
    n_col = B_HEADS * 2 * t
    assert n_col == LANE and page == LANE, "one score lane per (head, map, new token)"
    pps = 8
    while n_pages % pps:
        pps //= 2
    tok = pl.BlockSpec((t, w), lambda b, p, pt: (b, 0))
    ck = cache_k.reshape(n_phys, page * B_HEADS, B_DV)
    cv = cache_v.reshape(n_phys, page * B_HEADS, B_DV)
    pspec = lambda i: pl.BlockSpec((1, page * B_HEADS, B_DV), lambda b, p, pt, i=i: (pt[b, p * pps + i], 0, 0))
    grid_spec = pltpu.PrefetchScalarGridSpec(
        num_scalar_prefetch=1,
        grid=(nb, n_pages // pps),
        in_specs=[
            pl.BlockSpec(memory_space=pltpu.SMEM),
            pl.BlockSpec((4, B_DH), lambda b, p, pt: (0, 0)),
            pl.BlockSpec((1, B_DV), lambda b, p, pt: (0, 0)),
            tok, tok, tok,
        ] + [pspec(i) for i in range(pps)] + [pspec(i) for i in range(pps)],
        out_specs=pl.BlockSpec((t, w), lambda b, p, pt: (b, 0)),
        scratch_shapes=[
            pltpu.VMEM((3, page, n_col), F32),
            pltpu.VMEM((B_HEADS // 2, 2 * LANE, n_col), BF16),
            pltpu.VMEM((1, n_col), F32),
            pltpu.VMEM((1, n_col), F32),
            pltpu.VMEM((B_HEADS, 2 * t, B_DV), F32),
        ],
    )
    return pl.pallas_call(
        functools.partial(_attn_decode_kernel, pages_per_step=pps),
        grid_spec=grid_spec,
        out_shape=jax.ShapeDtypeStruct((nb * t, w), F32),
        compiler_params=_cparams(("arbitrary", "arbitrary")),
        name="attn_decode",
    )(page_table, rel_bias, b_lambda, og.reshape(1, B_DV), qn, kn, vn, *([ck] * pps), *([cv] * pps))


def _merge_kernel(ap_ref, as_ref, bp_ref, bs_ref, za0_ref, za1_ref, zb0_ref, zb1_ref, wa_ref, wb_ref, o_ref, *,
                  prompt_tiles):
    is_prompt = pl.program_id(0) < prompt_tiles
    a = jnp.where(is_prompt, ap_ref[...], as_ref[...]).astype(BF16)
    b = jnp.where(is_prompt, bp_ref[...], bs_ref[...]).astype(BF16)
    ya = jnp.dot(a, wa_ref[...], preferred_element_type=F32)
    yb = jnp.dot(b, wb_ref[...], preferred_element_type=F32)
    half = za0_ref.shape[1]
    for c, (za_ref, zb_ref) in enumerate(((za0_ref, zb0_ref), (za1_ref, zb1_ref))):
        cs = slice(c * half, (c + 1) * half)
        o_ref[:, cs] = (jax.nn.sigmoid(za_ref[...]) * ya[:, cs] + jax.nn.sigmoid(zb_ref[...]) * yb[:, cs]).astype(BF16)


def _merge(a_p, a_s, b_p, b_s, z, wa, wb):
    n_p, n_s = a_p.shape[0], a_s.shape[0]
    wa_in, d = wa.shape
    half = d // 2
    tm = next(t for t in (256, 128) if n_p % t == 0 and n_s % t == 0)
    pt = n_p // tm
    prow = lambda w: pl.BlockSpec((tm, w), lambda i: (jnp.minimum(i, pt - 1), 0))
    srow = lambda w: pl.BlockSpec((tm, w), lambda i: (jnp.maximum(i - pt, 0), 0))
    zcol = lambda off: pl.BlockSpec((tm, half), lambda i, off=off: (i, off // half))
    return pl.pallas_call(
        functools.partial(_merge_kernel, prompt_tiles=pt),
        grid=((n_p + n_s) // tm,),
        in_specs=[
            prow(wa_in), srow(wa_in), prow(wb.shape[0]), srow(wb.shape[0]),
            zcol(OFF_GA), zcol(OFF_GA + half), zcol(OFF_GB), zcol(OFF_GB + half),
            pl.BlockSpec(wa.shape, lambda i: (0, 0)),
            pl.BlockSpec(wb.shape, lambda i: (0, 0)),
        ],
        out_specs=pl.BlockSpec((tm, d), lambda i: (i, 0)),
        out_shape=jax.ShapeDtypeStruct((n_p + n_s, d), BF16),
        compiler_params=_cparams(("arbitrary",)),
        name="gated_merge",
    )(a_p, a_s, b_p, b_s, z, z, z, z, wa, wb)


def _out_proj_kernel(x_ref, m_ref, w_ref, g_ref, x1_ref, h2_ref):
    x1 = x_ref[...] + jnp.dot(m_ref[...], w_ref[...], preferred_element_type=F32)
    x1_ref[...] = x1
    ms = jnp.mean(x1 * x1, axis=-1, keepdims=True)
    h2_ref[...] = (x1 * lax.rsqrt(ms + EPS) * g_ref[...]).astype(BF16)


def _out_proj(x, merged, w_out, g2):
    n, d = x.shape
    tm = _row_tile(n, (256, 128))
    row = pl.BlockSpec((tm, d), lambda i: (i, 0))
    return pl.pallas_call(
        _out_proj_kernel,
        grid=(n // tm,),
        in_specs=[row, row, pl.BlockSpec((d, d), lambda i: (0, 0)), pl.BlockSpec((1, d), lambda i: (0, 0))],
        out_specs=[row, row],
        out_shape=[jax.ShapeDtypeStruct((n, d), F32), jax.ShapeDtypeStruct((n, d), BF16)],
        compiler_params=_cparams(("arbitrary",)),
        name="out_proj_norm2",
    )(x, merged, w_out, g2.reshape(1, d))


def _peer_scores_kernel(h_ref, wq_ref, k1_ref, k2_ref, s_ref):
    q = jnp.dot(h_ref[...], wq_ref[...], preferred_element_type=F32).astype(BF16)
    half = q.shape[1] // (2 * PEER_HEADS)
    for h in range(PEER_HEADS):
        for c, kr in enumerate((k1_ref, k2_ref)):
            qs = q[:, (2 * h + c) * half:(2 * h + c + 1) * half]
            s_ref[2 * h + c] = lax.dot_general(kr[h], qs, _NT, preferred_element_type=F32)


def _peer_scores(h2, wq, k1, k2):
    n, d = h2.shape
    tm = _row_tile(n, (256, 128))
    kspec = pl.BlockSpec(k1.shape, lambda i: (0, 0, 0))
    return pl.pallas_call(
        _peer_scores_kernel,
        grid=(n // tm,),
        in_specs=[pl.BlockSpec((tm, d), lambda i: (i, 0)), pl.BlockSpec(wq.shape, lambda i: (0, 0)), kspec, kspec],
        out_specs=pl.BlockSpec((2 * PEER_HEADS, PEER_NKEYS, tm), lambda i: (0, 0, i)),
        out_shape=jax.ShapeDtypeStruct((2 * PEER_HEADS, PEER_NKEYS, n), F32),
        compiler_params=_cparams(("arbitrary",)),
        name="peer_scores",
    )(h2, wq, k1, k2)


def _cand_layout():
    k = PEER_TOPK
    groups = [(0, k)] + [(j, 8) for j in range(1, 8)]
    pos = []
    for j, rows in groups:
        for l in range(rows):
            pos.append(j * k + l if (j + 1) * (l + 1) <= k else 1 << 20)
    pos += [j * k for j in range(8, k)]
    return groups, pos


def _topk_rows(s, k):
    n_rows = s.shape[0]
    row = lax.broadcasted_iota(I32, s.shape, 0).astype(F32)
    vals, idxs = [], []
    for _ in range(k):
        m = jnp.max(s, axis=0, keepdims=True)
        idx = jnp.min(jnp.where(s == m, row, float(n_rows)), axis=0, keepdims=True)
        vals.append(m)
        idxs.append(idx)
        s = jnp.where(row == idx, NEG_INF, s)
    return jnp.concatenate(vals, axis=0), jnp.concatenate(idxs, axis=0)


def _peer_topk_kernel(s_ref, pos_ref, i1_ref, i2_ref, g_ref):
    k = PEER_TOPK
    groups, _ = _cand_layout()
    pos = pos_ref[...]
    big = float(1 << 20)
    e_rows, g_rows = [], []
    for h in range(PEER_HEADS):
        v1, i1 = _topk_rows(s_ref[2 * h], k)
        v2, i2 = _topk_rows(s_ref[2 * h + 1], k)
        cand, cidx = [], []
        for j, rows in groups:
            cand.append(v1[j:j + 1] + v2[0:rows])
            cidx.append(i1[j:j + 1] * PEER_NKEYS + i2[0:rows])
        cand.append(v1[8:k] + v2[0:1])
        cidx.append(i1[8:k] * PEER_NKEYS + i2[0:1])
        cand = jnp.where(pos < big, jnp.concatenate(cand, axis=0), NEG_INF)
        cidx = jnp.concatenate(cidx, axis=0)
        sc, ex = [], []
        for _ in range(k):
            m = jnp.max(cand, axis=0, keepdims=True)
            pidx = jnp.min(jnp.where(cand == m, pos, big), axis=0, keepdims=True)
            sel = pos == pidx
            sc.append(m)
            ex.append(jnp.max(jnp.where(sel, cidx, -1.0), axis=0, keepdims=True))
            cand = jnp.where(sel, NEG_INF, cand)
        sc = jnp.concatenate(sc, axis=0)
        ew = jnp.exp(sc - sc[0:1])
        g_rows.append(ew / jnp.sum(ew, axis=0, keepdims=True))
        e_rows.append(jnp.concatenate(ex, axis=0))
    e_all = jnp.concatenate(e_rows, axis=0)
    g_all = jnp.concatenate(g_rows, axis=0)
    first = jnp.floor(e_all * (1.0 / PEER_NKEYS))
    i1_ref[...] = first.T
    i2_ref[...] = (e_all - first * PEER_NKEYS).T
    g_ref[...] = g_all.T


def _peer_topk(scores_t):
    n = scores_t.shape[2]
    tl = LANE
    _, pos = _cand_layout()
    pos_tab = jnp.broadcast_to(jnp.asarray(pos, F32)[:, None], (len(pos), tl))
    nsel = PEER_HEADS * PEER_TOPK
    ospec = pl.BlockSpec((tl, nsel), lambda i: (i, 0))
    oshape = jax.ShapeDtypeStruct((n, nsel), F32)
    return pl.pallas_call(
        _peer_topk_kernel,
        grid=(n // tl,),
        in_specs=[pl.BlockSpec((2 * PEER_HEADS, PEER_NKEYS, tl), lambda i: (0, 0, i)),
                  pl.BlockSpec(pos_tab.shape, lambda i: (0, 0))],
        out_specs=[ospec, ospec, ospec],
        out_shape=[oshape, oshape, oshape],
        compiler_params=_cparams(("arbitrary",)),
        name="peer_topk",
    )(scores_t, pos_tab)


def _peer_gates_kernel(i1_ref, i2_ref, g_ref, o_ref, x_scr):
    tb = i1_ref.shape[0]
    grp = 16
    sub = lax.broadcasted_iota(I32, (PEER_NKEYS, LANE), 0).astype(F32)
    ro = lax.broadcasted_iota(I32, (grp * grp, grp * grp), 0)
    ci = lax.broadcasted_iota(I32, (grp * grp, grp * grp), 1)
    swap = jnp.where((ro % grp) * grp + ro // grp == ci, 1.0, 0.0).astype(BF16)

    def token_gates(r0, slot):
        for t in range(grp):
            i1 = i1_ref[pl.ds(r0 + t, 1), :]
            i2 = i2_ref[pl.ds(r0 + t, 1), :]
            g = g_ref[pl.ds(r0 + t, 1), :]
            p1 = jnp.where(i1 == sub, g, 0.0).astype(BF16)
            p2 = jnp.where(i2 == sub, 1.0, 0.0).astype(BF16)
            x_scr[slot, t * PEER_NKEYS:(t + 1) * PEER_NKEYS, :] = lax.dot_general(
                p1, p2, _NT, preferred_element_type=F32).astype(BF16)

    def regroup(r0, slot):
        for kb in range(PEER_NKEYS // grp):
            rows = [x_scr[slot, t * PEER_NKEYS + kb * grp:t * PEER_NKEYS + (kb + 1) * grp, :] for t in range(grp)]
            w = jnp.dot(swap, jnp.concatenate(rows, axis=0), preferred_element_type=F32)
            for j in range(grp):
                o_ref[kb * grp + j, pl.ds(r0, grp), :] = w[j * grp:(j + 1) * grp].astype(BF16)

    n_slot = x_scr.shape[0]

    def groups(gi, carry):
        r0 = pl.multiple_of(gi * (n_slot * grp), n_slot * grp)
        for slot in range(n_slot):
            token_gates(r0 + slot * grp, slot)
        for slot in range(n_slot):
            regroup(r0 + slot * grp, slot)
        return carry

    lax.fori_loop(0, tb // (n_slot * grp), groups, 0)


def _peer_gates(i1, i2, g):
    n, nsel = i1.shape
    tb = LANE
    ispec = pl.BlockSpec((tb, nsel), lambda i: (i, 0))
    return pl.pallas_call(
        _peer_gates_kernel,
        grid=(n // tb,),
        in_specs=[ispec, ispec, ispec],
        out_specs=pl.BlockSpec((PEER_NKEYS, tb, PEER_NKEYS), lambda i: (0, i, 0)),
        out_shape=jax.ShapeDtypeStruct((PEER_NKEYS, n, PEER_NKEYS), BF16),
        scratch_shapes=[pltpu.VMEM((4, 16 * PEER_NKEYS, PEER_NKEYS), BF16)],
        compiler_params=_cparams(("arbitrary",)),
        name="peer_gates",
    )(i1, i2, g)


def _peer_dense_kernel(h_ref, x1_ref, u_ref, v_ref, g_ref, o_ref):
    @pl.when(pl.program_id(1) == 0)
    def _():
        o_ref[...] = x1_ref[...]

    a = lax.dot_general(h_ref[...], u_ref[...], _NT, preferred_element_type=F32)
    ws = []
    for c in range(g_ref.shape[0]):
        ac = a[:, c * LANE:(c + 1) * LANE]
        gelu = 0.5 * ac * (1.0 + lax.erf(ac * SQRT_HALF))
        ws.append((gelu * g_ref[c].astype(F32)).astype(BF16))
    w = jnp.concatenate(ws, axis=1)
    o_ref[...] += jnp.dot(w, v_ref[...], preferred_element_type=F32)


def _peer_dense(h2, x1, u, v, gates):
    n, d = h2.shape
    n_exp = u.shape[0]
    tm = _row_tile(n)
    te = 512
    ge = te // PEER_NKEYS
    row = lambda i, e: (i, 0)
    return pl.pallas_call(
        _peer_dense_kernel,
        grid=(n // tm, n_exp // te),
        in_specs=[
            pl.BlockSpec((tm, d), row),
            pl.BlockSpec((tm, d), row),
            pl.BlockSpec((te, d), lambda i, e: (e, 0)),
            pl.BlockSpec((te, d), lambda i, e: (e, 0)),
            pl.BlockSpec((ge, tm, PEER_NKEYS), lambda i, e: (e, i, 0)),
        ],
        out_specs=pl.BlockSpec((tm, d), row),
        out_shape=jax.ShapeDtypeStruct((n, d), F32),
        compiler_params=_cparams(("arbitrary", "arbitrary")),
        name="peer_dense",
    )(h2, x1, u, v, gates)


def kernel(x_prompt, x_sample, cache_k, cache_v, state_hgrn, page_table, meta_tokens, lb_logits, rel_bias,
           norm1_g, w_in, a_onorm_g, a_proj, b_qnorm_g, b_knorm_g, b_lambda, b_onorm_g, b_proj, w_out,
           norm2_g, peer_wq, peer_k1, peer_k2, peer_u, peer_v):
    assert w_in.shape[0] == 1, "one layer"
    nb, seq, d = x_prompt.shape
    nbs, t = x_sample.shape[:2]
    lp = seq + LANE
    n_p, n_s = nb * lp, nbs * t
    n = n_p + n_s

    front = jnp.concatenate([jnp.zeros((PAD_FRONT, d), F32), meta_tokens.astype(F32)], axis=0)
    pieces = []
    for b in range(nb):
        pieces += [front, x_prompt[b]]
    x = jnp.concatenate(pieces + [x_sample.reshape(n_s, d)], axis=0)

    bf = lambda w: w.astype(BF16)
    z = _rms_matmul(x, norm1_g[0], bf(w_in[0]), 1024)

    a_p, s_p = _hgrn(z, lb_logits, a_onorm_g[0], jnp.zeros((nb, A_HEADS, A_DK, A_DV), F32),
                     row0=0, nb=nb, t_len=lp, c=HGRN_CHUNK, hp=HGRN_HEADS_PER_STEP)
    a_s, s_s = _hgrn(z, lb_logits, a_onorm_g[0], state_hgrn[0], row0=n_p, nb=nbs, t_len=t, c=t, hp=A_HEADS)

    qn, qt, kn, knb, vt = _qk_norm(z, b_qnorm_g[0], b_knorm_g[0])
    b_p = _attn_prompt(qt, knb, vt, rel_bias, b_lambda[0], b_onorm_g[0], nb=nb, lp=lp)
    w = B_HEADS * B_DV
    b_s = _attn_decode(qn[n_p:].astype(F32), kn[n_p:], z[n_p:, OFF_BV:OFF_BV + w], cache_k, cache_v, page_table,
                       rel_bias, b_lambda[0], b_onorm_g[0], nb=nbs, t=t)

    merged = _merge(a_p, a_s, b_p, b_s, z, bf(a_proj[0]), bf(b_proj[0]))
    x1, h2 = _out_proj(x, merged, bf(w_out[0]), norm2_g[0])

    scores_t = _peer_scores(h2, bf(peer_wq[0]), bf(peer_k1[0]), bf(peer_k2[0]))
    i1, i2, g = _peer_topk(scores_t)
    gates = _peer_gates(i1, i2, g)
    y = _peer_dense(h2, x1, bf(peer_u[0]), bf(peer_v[0]), gates)

    y_prompt = y[:n_p].reshape(nb, lp, d)[:, LANE:]
    y_sample = y[n_p:].reshape(nbs, t, d)
    k_prompt = kn[:n_p].reshape(nb, lp, B_HEADS, 2 * B_DH)[:, PAD_FRONT:][None]
    v_prompt = z[:n_p, OFF_BV:OFF_BV + w].reshape(nb, lp, B_HEADS, B_DV)[:, PAD_FRONT:][None]
    k_sample = kn[n_p:].reshape(nbs, t, B_HEADS, 2 * B_DH)[None]
    v_sample = z[n_p:, OFF_BV:OFF_BV + w].reshape(nbs, t, B_HEADS, B_DV)[None]
    return (y_prompt, y_sample, k_prompt, v_prompt, s_p[None], k_sample, v_sample, s_s[None])
```

```python
import functools
import math

import jax
import jax.numpy as jnp
from jax import lax
from jax.experimental import pallas as pl
from jax.experimental.pallas import tpu as pltpu

F32 = jnp.float32
BF16 = jnp.bfloat16
I32 = jnp.int32

LANE = 128
D_MODEL = 2048
N_META = 16
PAD_FRONT = LANE - N_META
A_HEADS = 8
A_DK = 128
A_DV = 128
B_HEADS = 8
B_DH = 64
B_DV = 128
N_BUCKETS = 32
MAX_DISTANCE = 128
PEER_HEADS = 8
PEER_NKEYS = 128
PEER_TOPK = 16
EPS = 1e-6
NEG_INF = -1e30
ATTN_SCALE = B_DH ** -0.5
SQRT_HALF = 0.7071067811865476
HGRN_CHUNK = 16
HGRN_HEADS_PER_STEP = 4
VMEM_LIMIT = 56 * 1024 * 1024

OFF_AQ, OFF_AF, OFF_AI, OFF_AG = 0, 1024, 2048, 3072
OFF_BQ, OFF_BK, OFF_BV = 4096, 5120, 6144
OFF_GA, OFF_GB = 7168, 9216
IN_TOTAL = 11264

_NT = (((1,), (1,)), ((), ()))
_TN = (((0,), (0,)), ((), ()))


def _cparams(sem):
    return pltpu.CompilerParams(dimension_semantics=sem, vmem_limit_bytes=VMEM_LIMIT)


def _row_tile(n, prefs=(768, 512, 384, 256, 128)):
    for t in prefs:
        if n % t == 0:
            return t
    raise ValueError(f"token count {n} is not a multiple of 128")


def _rms_matmul_kernel(x_ref, g_ref, w_ref, o_ref, h_ref):
    @pl.when(pl.program_id(1) == 0)
    def _():
        x = x_ref[...]
        ms = jnp.mean(x * x, axis=-1, keepdims=True)
        h_ref[...] = (x * lax.rsqrt(ms + EPS) * g_ref[...]).astype(BF16)

    o_ref[...] = jnp.dot(h_ref[...], w_ref[...], preferred_element_type=F32)


def _rms_matmul(x, g, w_bf16, tn):
    n, d = x.shape
    nc = w_bf16.shape[1]
    tm = _row_tile(n)
    return pl.pallas_call(
        _rms_matmul_kernel,
        grid=(n // tm, nc // tn),
        in_specs=[
            pl.BlockSpec((tm, d), lambda i, j: (i, 0)),
            pl.BlockSpec((1, d), lambda i, j: (0, 0)),
            pl.BlockSpec((d, tn), lambda i, j: (0, j)),
        ],
        out_specs=pl.BlockSpec((tm, tn), lambda i, j: (i, j)),
        out_shape=jax.ShapeDtypeStruct((n, nc), F32),
        scratch_shapes=[pltpu.VMEM((tm, d), BF16)],
        compiler_params=_cparams(("arbitrary", "arbitrary")),
        name="rms_in_proj",
    )(x, g.reshape(1, d), w_bf16)


def _hgrn_kernel(lbl_ref, gn_ref, q_ref, f_ref, i_ref, g_ref, s0_ref, o_ref, sfin_ref, st_scr, *, c, n_chunks, hp):
    for hh in range(hp):
        st_scr[hh] = s0_ref[0, hh].T
    lg = lbl_ref[...]
    e = jnp.exp(lg - jnp.max(lg, axis=0, keepdims=True))
    lb_all = e[0:1] / jnp.sum(e, axis=0, keepdims=True)
    gn = gn_ref[...]
    row = lax.broadcasted_iota(I32, (c, A_DK), 0)
    ones = jnp.ones((A_DK, A_DV), BF16)

    def chunk(ci, carry):
        r0 = pl.multiple_of(ci * c, c)
        for hh in range(hp):
            head_chunk(r0, hh)
        return carry

    def head_chunk(r0, hh):
        sl = slice(hh * A_DK, (hh + 1) * A_DK)
        lb = lb_all[:, sl]
        q = q_ref[pl.ds(r0, c), sl]
        zf = f_ref[pl.ds(r0, c), sl]
        v = i_ref[pl.ds(r0, c), sl]
        zg = g_ref[pl.ds(r0, c), sl]
        f = lb + (1.0 - lb) * jax.nn.sigmoid(zf)
        k = 1.0 - f
        b = jnp.log(f)
        sh = 1
        while sh < c:
            b = b + jnp.where(row >= sh, pltpu.roll(b, sh, 0), 0.0)
            sh *= 2
        sub = 8
        ps = []
        for s in range(c):
            t0 = s // sub * sub
            d = jnp.exp(jnp.where(row[t0:] >= s, b[t0:] - b[s:s + 1, :], NEG_INF))
            ps.append(q[t0:] * d * k[s:s + 1, :])
        p = jnp.concatenate(ps, axis=0).astype(BF16)
        a = jnp.dot(p, ones, preferred_element_type=F32)
        o_groups = [None] * (c // sub)
        off = 0
        for s in range(c):
            t0 = s // sub * sub
            for gi in range(t0 // sub, c // sub):
                r0_ = off + gi * sub - t0
                piece = a[r0_:r0_ + sub] * v[s:s + 1, :]
                o_groups[gi] = piece if o_groups[gi] is None else o_groups[gi] + piece
            off += c - t0
        o = jnp.concatenate(o_groups, axis=0)
        st = st_scr[hh]
        qe = (q * jnp.exp(b)).astype(BF16)
        o = o + lax.dot_general(qe, st.astype(BF16), _NT, preferred_element_type=F32)
        bc = b[c - 1:c, :]
        kt = (k * jnp.exp(bc - b)).astype(BF16)
        upd = lax.dot_general(v.astype(BF16), kt, _TN, preferred_element_type=F32)
        st_scr[hh] = st * jnp.exp(bc) + upd
        ms = jnp.mean(o * o, axis=-1, keepdims=True)
        o_ref[pl.ds(r0, c), sl] = o * lax.rsqrt(ms + EPS) * gn * (zg * jax.nn.sigmoid(zg))

    lax.fori_loop(0, n_chunks, chunk, 0, unroll=4 if n_chunks % 4 == 0 else 1)
    for hh in range(hp):
        sfin_ref[0, hh] = st_scr[hh].T


def _hgrn(z, lb_logits, gn, s0, *, row0, nb, t_len, c, hp):
    assert row0 % t_len == 0 and t_len % c == 0 and A_HEADS % hp == 0
    rb0 = row0 // t_len
    wb = hp * A_DK
    zspec = lambda off: pl.BlockSpec((t_len, wb), lambda b, h, off=off: (rb0 + b, off // wb + h))
    return pl.pallas_call(
        functools.partial(_hgrn_kernel, c=c, n_chunks=t_len // c, hp=hp),
        grid=(nb, A_HEADS // hp),
        in_specs=[
            pl.BlockSpec((lb_logits.shape[0], wb), lambda b, h: (0, h)),
            pl.BlockSpec((1, A_DV), lambda b, h: (0, 0)),
            zspec(OFF_AQ), zspec(OFF_AF), zspec(OFF_AI), zspec(OFF_AG),
            pl.BlockSpec((1, hp, A_DK, A_DV), lambda b, h: (b, h, 0, 0)),
        ],
        out_specs=[
            pl.BlockSpec((t_len, wb), lambda b, h: (b, h)),
            pl.BlockSpec((1, hp, A_DK, A_DV), lambda b, h: (b, h, 0, 0)),
        ],
        out_shape=[
            jax.ShapeDtypeStruct((nb * t_len, A_HEADS * A_DV), F32),
            jax.ShapeDtypeStruct((nb, A_HEADS, A_DK, A_DV), F32),
        ],
        scratch_shapes=[pltpu.VMEM((hp, A_DV, A_DK), F32)],
        compiler_params=_cparams(("arbitrary", "arbitrary")),
        name=f"hgrn_c{c}",
    )(lb_logits, gn.reshape(1, A_DV), z, z, z, z, s0)


def _qk_norm_kernel(zq_ref, zk_ref, zv_ref, qg_ref, kg_ref, qn_ref, qt_ref, kn_ref, knb_ref, vt_ref):
    r = lax.broadcasted_iota(I32, (LANE, LANE), 0) // B_DH
    cc = lax.broadcasted_iota(I32, (LANE, LANE), 1) // B_DH
    seg = jnp.where(r == cc, 1.0, 0.0).astype(BF16)

    def norm(z, g):
        sq = z * z
        hi = sq.astype(BF16)
        lo = (sq - hi.astype(F32)).astype(BF16)
        ss = jnp.dot(hi, seg, preferred_element_type=F32) + jnp.dot(lo, seg, preferred_element_type=F32)
        return z * lax.rsqrt(ss * (1.0 / B_DH) + EPS) * g

    qg = qg_ref[...]
    kg = kg_ref[...]
    for h in range(B_HEADS):
        sl = slice(h * LANE, (h + 1) * LANE)
        qn = norm(zq_ref[:, sl], qg) * ATTN_SCALE
        qn_ref[:, sl] = qn.astype(BF16)
        qt_ref[sl, :] = qn.T.astype(BF16)
        kn = norm(zk_ref[:, sl], kg)
        kn_ref[:, sl] = kn
        knb_ref[:, sl] = kn.astype(BF16)
        vt_ref[sl, :] = zv_ref[:, sl].T.astype(BF16)


def _qk_norm(z, qg, kg):
    n = z.shape[0]
    tm = _row_tile(n)
    w = B_HEADS * 2 * B_DH
    zspec = lambda off: pl.BlockSpec((tm, w), lambda i, off=off: (i, off // w))
    ospec = pl.BlockSpec((tm, w), lambda i: (i, 0))
    tspec = pl.BlockSpec((w, tm), lambda i: (0, i))
    gspec = pl.BlockSpec((1, LANE), lambda i: (0, 0))
    return pl.pallas_call(
        _qk_norm_kernel,
        grid=(n // tm,),
        in_specs=[zspec(OFF_BQ), zspec(OFF_BK), zspec(OFF_BV), gspec, gspec],
        out_specs=[ospec, tspec, ospec, ospec, tspec],
        out_shape=[
            jax.ShapeDtypeStruct((n, w), BF16),
            jax.ShapeDtypeStruct((w, n), BF16),
            jax.ShapeDtypeStruct((n, w), F32),
            jax.ShapeDtypeStruct((n, w), BF16),
            jax.ShapeDtypeStruct((w, n), BF16),
        ],
        compiler_params=_cparams(("arbitrary",)),
        name="qk_norm",
    )(z, z, z, jnp.tile(qg, 2).reshape(1, LANE), jnp.tile(kg, 2).reshape(1, LANE))


def _rel_bucket(dist):
    n = jnp.maximum(dist, 0)
    max_exact = N_BUCKETS // 2
    nf = jnp.maximum(n, 1).astype(F32)
    large = max_exact + (jnp.log(nf / max_exact) / math.log(MAX_DISTANCE / max_exact)
                         * (N_BUCKETS - max_exact)).astype(I32)
    large = jnp.minimum(large, N_BUCKETS - 1)
    return jnp.where(n < max_exact, n, large)


def _bias_from_bucket(bucket, rb_ref, h):
    val = jnp.zeros(bucket.shape, F32)
    for bk in range(N_BUCKETS):
        val = jnp.where(bucket == bk, rb_ref[bk, h], val)
    return val


def _diff_lambda(lam_ref):
    lp = lam_ref[...]
    lam_init = 0.8 - 0.6 * math.exp(-0.3 * 0)
    lam = (jnp.exp(jnp.sum(lp[0:1] * lp[1:2], axis=-1, keepdims=True))
           - jnp.exp(jnp.sum(lp[2:3] * lp[3:4], axis=-1, keepdims=True)) + lam_init)
    return lam, lam_init


def _split_maps(qh):
    lane = lax.broadcasted_iota(I32, qh.shape, 1)
    zero = jnp.zeros_like(qh)
    return jnp.concatenate([jnp.where(lane < B_DH, qh, zero), jnp.where(lane >= B_DH, qh, zero)], axis=0)


def _online_update(carry, s, vj):
    m, l, acc = carry
    m_new = jnp.maximum(m, jnp.max(s, axis=-1, keepdims=True))
    a = jnp.exp(m - m_new)
    p = jnp.exp(s - m_new)
    l = l * a + jnp.sum(p, axis=-1, keepdims=True)
    acc = acc * a + jnp.dot(p.astype(BF16), vj, preferred_element_type=F32)
    return m_new, l, acc


def _diff_finish(carry, t, lam, lam_init, og):
    _, l, acc = carry
    o = acc[0:t] / l[0:t] - lam * (acc[t:2 * t] / l[t:2 * t])
    ms = jnp.mean(o * o, axis=-1, keepdims=True)
    return o * lax.rsqrt(ms + EPS) * og * (1.0 - lam_init)


_BIAS_DIAG, _BIAS_NEAR, _BIAS_DIAG_PAD, _BIAS_NEAR_PAD, _BIAS_FAR_PAD = range(5)


def _attn_prompt_kernel(rb_ref, lam_ref, og_ref, qt_ref, k_ref, vt_ref, o_ref, bias_scr, q2_scr, m_scr, l_scr, acc_scr):
    i = pl.program_id(1)
    tq = qt_ref.shape[1]

    @pl.when((pl.program_id(0) == 0) & (i == 0))
    def _():
        r = lax.broadcasted_iota(I32, (tq, 2 * tq), 0)
        qi = lax.broadcasted_iota(I32, (tq, 2 * tq), 1) % tq
        in_pad = r < PAD_FRONT
        for blk in range(2):
            d = qi - r + blk * tq
            bucket = _rel_bucket(d)
            for h in range(B_HEADS):
                val = _bias_from_bucket(bucket, rb_ref, h)
                if blk == 0:
                    val = jnp.where(d >= 0, val, NEG_INF)
                bias_scr[_BIAS_DIAG + blk, h] = val
                bias_scr[_BIAS_DIAG_PAD + blk, h] = jnp.where(in_pad, NEG_INF, val)
        for h in range(B_HEADS):
            bias_scr[_BIAS_FAR_PAD, h] = jnp.where(in_pad, NEG_INF, rb_ref[N_BUCKETS - 1, h])

    row = lax.broadcasted_iota(I32, (LANE, tq), 0)
    for h in range(B_HEADS):
        qt = qt_ref[h * LANE:(h + 1) * LANE, :]
        zero = jnp.zeros_like(qt)
        q2_scr[h] = jnp.concatenate([jnp.where(row < B_DH, qt, zero), jnp.where(row >= B_DH, qt, zero)], axis=1)
    m_scr[...] = jnp.full(m_scr.shape, NEG_INF, F32)
    l_scr[...] = jnp.zeros(l_scr.shape, F32)
    acc_scr[...] = jnp.zeros(acc_scr.shape, F32)

    def step(j, bias_of):
        k0 = pl.multiple_of(j * tq, tq)
        for h in range(B_HEADS):
            sl = slice(h * LANE, (h + 1) * LANE)
            s = jnp.dot(k_ref[pl.ds(k0, tq), sl], q2_scr[h], preferred_element_type=F32) + bias_of(h)
            m = m_scr[h]
            m_new = jnp.maximum(m, jnp.max(s, axis=0, keepdims=True))
            a = jnp.exp(m - m_new)
            p = jnp.exp(s - m_new)
            m_scr[h] = m_new
            l_scr[h] = l_scr[h] * a + jnp.sum(p, axis=0, keepdims=True)
            acc_scr[h] = acc_scr[h] * a + jnp.dot(vt_ref[sl, pl.ds(k0, tq)], p.astype(BF16),
                                                  preferred_element_type=F32)

    first = jnp.where(i == 0, _BIAS_DIAG_PAD, jnp.where(i == 1, _BIAS_NEAR_PAD, _BIAS_FAR_PAD))
    step(0, lambda h: bias_scr[first, h])

    def far_body(j, carry):
        step(j, lambda h: rb_ref[N_BUCKETS - 1, h])
        return carry

    lax.fori_loop(1, jnp.maximum(i - 1, 1), far_body, 0)

    @pl.when(i >= 2)
    def _():
        step(i - 1, lambda h: bias_scr[_BIAS_NEAR, h])

    @pl.when(i >= 1)
    def _():
        step(i, lambda h: bias_scr[_BIAS_DIAG, h])

    lam, lam_init = _diff_lambda(lam_ref)
    og = og_ref[...]
    for h in range(B_HEADS):
        acc = acc_scr[h]
        l = l_scr[h]
        o = (acc[:, :tq] / l[:, :tq] - lam * (acc[:, tq:] / l[:, tq:])).T
        ms = jnp.mean(o * o, axis=-1, keepdims=True)
        o_ref[:, h * LANE:(h + 1) * LANE] = o * lax.rsqrt(ms + EPS) * og * (1.0 - lam_init)


def _attn_prompt(qt, knb, vt, rel_bias, b_lambda, og, *, nb, lp):
    tq = LANE
    nq = lp // tq
    w = B_HEADS * B_DV
    return pl.pallas_call(
        _attn_prompt_kernel,
        grid=(nb, nq),
        in_specs=[
            pl.BlockSpec(memory_space=pltpu.SMEM),
            pl.BlockSpec((4, B_DH), lambda b, i: (0, 0)),
            pl.BlockSpec((1, B_DV), lambda b, i: (0, 0)),
            pl.BlockSpec((w, tq), lambda b, i: (0, b * nq + i)),
            pl.BlockSpec((lp, w), lambda b, i: (b, 0)),
            pl.BlockSpec((w, lp), lambda b, i: (0, b)),
        ],
        out_specs=pl.BlockSpec((tq, w), lambda b, i: (b * nq + i, 0)),
        out_shape=jax.ShapeDtypeStruct((nb * lp, w), F32),
        scratch_shapes=[
            pltpu.VMEM((5, B_HEADS, tq, 2 * tq), F32),
            pltpu.VMEM((B_HEADS, LANE, 2 * tq), BF16),
            pltpu.VMEM((B_HEADS, 1, 2 * tq), F32),
            pltpu.VMEM((B_HEADS, 1, 2 * tq), F32),
            pltpu.VMEM((B_HEADS, B_DV, 2 * tq), F32),
        ],
        compiler_params=_cparams(("arbitrary", "arbitrary")),
        name="attn_prompt",
    )(rel_bias, b_lambda, og.reshape(1, B_DV), qt, knb, vt)


def _attn_decode_kernel(pt_ref, rb_ref, lam_ref, og_ref, q_ref, kn_ref, vn_ref, *rest, pages_per_step):
    kp = rest[:pages_per_step]
    vp = rest[pages_per_step:2 * pages_per_step]
    o_ref, bias_scr, qbd_scr, m_scr, l_scr, acc_scr = rest[2 * pages_per_step:]
    del pt_ref
    b = pl.program_id(0)
    p = pl.program_id(1)
    n_steps = pl.num_programs(1)
    t = q_ref.shape[0]
    page = kp[0].shape[1] // B_HEADS
    rows = 2 * t
    n_col = B_HEADS * rows
    pair = 2 * LANE
    n_pair = B_HEADS // 2

    def head_rows(ref, h):
        return ref[0, pl.ds(h, page, stride=B_HEADS), :].astype(BF16)

    @pl.when((b == 0) & (p == 0))
    def _():
        key = lax.broadcasted_iota(I32, (page, n_col), 0)
        col = lax.broadcasted_iota(I32, (page, n_col), 1)
        tok = col % t
        bk_last = _rel_bucket(page - key + tok)
        d_self = tok - key
        bk_self = _rel_bucket(d_self)
        v_last = jnp.zeros((page, n_col), F32)
        v_self = jnp.zeros((page, n_col), F32)
        v_far = jnp.zeros((page, n_col), F32)
        for h in range(B_HEADS):
            mine = col // rows == h
            v_last = jnp.where(mine, _bias_from_bucket(bk_last, rb_ref, h), v_last)
            v_self = jnp.where(mine, _bias_from_bucket(bk_self, rb_ref, h), v_self)
            v_far = jnp.where(mine, rb_ref[N_BUCKETS - 1, h], v_far)
        bias_scr[0] = v_last
        bias_scr[1] = jnp.where((d_self >= 0) & (key < t), v_self, NEG_INF)
        bias_scr[2] = v_far

    @pl.when(p == 0)
    def _():
        m_scr[...] = jnp.full(m_scr.shape, NEG_INF, F32)
        l_scr[...] = jnp.zeros(l_scr.shape, F32)
        acc_scr[...] = jnp.zeros(acc_scr.shape, F32)
        w = B_HEADS * LANE
        qrep = jnp.concatenate([q_ref[...]] * (n_col // t), axis=0)
        r = lax.broadcasted_iota(I32, (n_col, w), 0)
        f = lax.broadcasted_iota(I32, (n_col, w), 1)
        keep = (f // LANE == r // rows) & ((f % LANE) // B_DH == (r // t) % 2)
        qbd = jnp.where(keep, qrep, 0.0)
        for g in range(n_pair):
            for hh in range(2):
                c0 = g * pair + hh * LANE
                qbd_scr[g, hh * LANE:(hh + 1) * LANE, :] = qbd[:, c0:c0 + LANE].T.astype(BF16)

    far_row = bias_scr[2, 0:1, :]

    def key_scores(get_pair):
        s = jnp.dot(get_pair(0), qbd_scr[0], preferred_element_type=F32)
        for g in range(1, n_pair):
            s = s + jnp.dot(get_pair(g), qbd_scr[g], preferred_element_type=F32)
        return s

    def online_update(s, values):
        st = jnp.concatenate(s, axis=0)
        m = m_scr[...]
        m_new = jnp.maximum(m, jnp.max(st, axis=0, keepdims=True))
        a = jnp.exp(m - m_new)
        pt_ = jnp.exp(st - m_new)
        m_scr[...] = m_new
        l_scr[...] = l_scr[...] * a + jnp.sum(pt_, axis=0, keepdims=True)
        a_col = jnp.broadcast_to(a, (n_col, n_col)).T
        pq = [pt_[i * page:(i + 1) * page].T.astype(BF16) for i in range(len(s))]
        for h in range(B_HEADS):
            hs = slice(h * rows, (h + 1) * rows)
            pv = jnp.dot(pq[0][hs], values[0](h), preferred_element_type=F32)
            for i in range(1, len(s)):
                pv = pv + jnp.dot(pq[i][hs], values[i](h), preferred_element_type=F32)
            acc_scr[h] = acc_scr[h] * a_col[hs] + pv

    blocks = []
    for i in range(pages_per_step):
        s = key_scores(lambda g, i=i: jnp.concatenate([head_rows(kp[i], 2 * g), head_rows(kp[i], 2 * g + 1)], axis=1))
        if i == pages_per_step - 1:
            blocks.append(s + jnp.where(p == n_steps - 1, bias_scr[0], far_row))
        else:
            blocks.append(s + far_row)
    online_update(blocks, [lambda h, i=i: head_rows(vp[i], h) for i in range(pages_per_step)])

    @pl.when(p == n_steps - 1)
    def _():
        lam, lam_init = _diff_lambda(lam_ref)
        og = og_ref[...]
        kn = jnp.concatenate([kn_ref[...].astype(BF16), jnp.zeros((page - t, B_HEADS * LANE), BF16)], axis=0)
        s = key_scores(lambda g: kn[:, g * pair:(g + 1) * pair]) + bias_scr[1]
        zpad = jnp.zeros((page - t, LANE), BF16)
        online_update([s], [lambda h: jnp.concatenate([vn_ref[:, h * LANE:(h + 1) * LANE].astype(BF16), zpad], axis=0)])
        l_col = jnp.broadcast_to(l_scr[...], (n_col, n_col)).T
        for h in range(B_HEADS):
            hs = slice(h * rows, (h + 1) * rows)
            o_ref[:, h * LANE:(h + 1) * LANE] = _diff_finish((None, l_col[hs], acc_scr[h]), t, lam, lam_init, og)


def _attn_decode(qn, kn, vn, cache_k, cache_v, page_table, rel_bias, b_lambda, og, *, nb, t):
    n_pages = page_table.shape[1]
    n_phys, page = cache_k.shape[1], cache_k.shape[2]
    w = B_HEADS * B_DV
    n_col = B_HEADS * 2 * t
    assert n_col == LANE and page == LANE, "one score lane per (head, map, new token)"
    pps = 8
    while n_pages % pps:
        pps //= 2
    tok = pl.BlockSpec((t, w), lambda b, p, pt: (b, 0))
    ck = cache_k.reshape(n_phys, page * B_HEADS, B_DV)
    cv = cache_v.reshape(n_phys, page * B_HEADS, B_DV)
    pspec = lambda i: pl.BlockSpec((1, page * B_HEADS, B_DV), lambda b, p, pt, i=i: (pt[b, p * pps + i], 0, 0))
    grid_spec = pltpu.PrefetchScalarGridSpec(
        num_scalar_prefetch=1,
        grid=(nb, n_pages // pps),
        in_specs=[
            pl.BlockSpec(memory_space=pltpu.SMEM),
            pl.BlockSpec((4, B_DH), lambda b, p, pt: (0, 0)),
            pl.BlockSpec((1, B_DV), lambda b, p, pt: (0, 0)),
            tok, tok, tok,
        ] + [pspec(i) for i in range(pps)] + [pspec(i) for i in range(pps)],
        out_specs=pl.BlockSpec((t, w), lambda b, p, pt: (b, 0)),
        scratch_shapes=[
            pltpu.VMEM((3, page, n_col), F32),
            pltpu.VMEM((B_HEADS // 2, 2 * LANE, n_col), BF16),
            pltpu.VMEM((1, n_col), F32),
            pltpu.VMEM((1, n_col), F32),
            pltpu.VMEM((B_HEADS, 2 * t, B_DV), F32),
        ],
    )
    return pl.pallas_call(
        functools.partial(_attn_decode_kernel, pages_per_step=pps),
        grid_spec=grid_spec,
        out_shape=jax.ShapeDtypeStruct((nb * t, w), F32),
        compiler_params=_cparams(("arbitrary", "arbitrary")),
        name="attn_decode",
    )(page_table, rel_bias, b_lambda, og.reshape(1, B_DV), qn, kn, vn, *([ck] * pps), *([cv] * pps))


def _merge_kernel(ap_ref, as_ref, bp_ref, bs_ref, za0_ref, za1_ref, zb0_ref, zb1_ref, wa_ref, wb_ref, o_ref, *,
                  prompt_tiles):
    is_prompt = pl.program_id(0) < prompt_tiles
    a = jnp.where(is_prompt, ap_ref[...], as_ref[...]).astype(BF16)
    b = jnp.where(is_prompt, bp_ref[...], bs_ref[...]).astype(BF16)
    ya = jnp.dot(a, wa_ref[...], preferred_element_type=F32)
    yb = jnp.dot(b, wb_ref[...], preferred_element_type=F32)
    half = za0_ref.shape[1]
    for c, (za_ref, zb_ref) in enumerate(((za0_ref, zb0_ref), (za1_ref, zb1_ref))):
        cs = slice(c * half, (c + 1) * half)
        o_ref[:, cs] = (jax.nn.sigmoid(za_ref[...]) * ya[:, cs] + jax.nn.sigmoid(zb_ref[...]) * yb[:, cs]).astype(BF16)


def _merge(a_p, a_s, b_p, b_s, z, wa, wb):
    n_p, n_s = a_p.shape[0], a_s.shape[0]
    wa_in, d = wa.shape
    half = d // 2
    tm = next(t for t in (256, 128) if n_p % t == 0 and n_s % t == 0)
    pt = n_p // tm
    prow = lambda w: pl.BlockSpec((tm, w), lambda i: (jnp.minimum(i, pt - 1), 0))
    srow = lambda w: pl.BlockSpec((tm, w), lambda i: (jnp.maximum(i - pt, 0), 0))
    zcol = lambda off: pl.BlockSpec((tm, half), lambda i, off=off: (i, off // half))
    return pl.pallas_call(
        functools.partial(_merge_kernel, prompt_tiles=pt),
        grid=((n_p + n_s) // tm,),
        in_specs=[
            prow(wa_in), srow(wa_in), prow(wb.shape[0]), srow(wb.shape[0]),
            zcol(OFF_GA), zcol(OFF_GA + half), zcol(OFF_GB), zcol(OFF_GB + half),
            pl.BlockSpec(wa.shape, lambda i: (0, 0)),
            pl.BlockSpec(wb.shape, lambda i: (0, 0)),
        ],
        out_specs=pl.BlockSpec((tm, d), lambda i: (i, 0)),
        out_shape=jax.ShapeDtypeStruct((n_p + n_s, d), BF16),
        compiler_params=_cparams(("arbitrary",)),
        name="gated_merge",
    )(a_p, a_s, b_p, b_s, z, z, z, z, wa, wb)


def _out_proj_kernel(x_ref, m_ref, w_ref, g_ref, x1_ref, h2_ref):
    x1 = x_ref[...] + jnp.dot(m_ref[...], w_ref[...], preferred_element_type=F32)
    x1_ref[...] = x1
    ms = jnp.mean(x1 * x1, axis=-1, keepdims=True)
    h2_ref[...] = (x1 * lax.rsqrt(ms + EPS) * g_ref[...]).astype(BF16)


def _out_proj(x, merged, w_out, g2):
    n, d = x.shape
    tm = _row_tile(n, (256, 128))
    row = pl.BlockSpec((tm, d), lambda i: (i, 0))
    return pl.pallas_call(
        _out_proj_kernel,
        grid=(n // tm,),
        in_specs=[row, row, pl.BlockSpec((d, d), lambda i: (0, 0)), pl.BlockSpec((1, d), lambda i: (0, 0))],
        out_specs=[row, row],
        out_shape=[jax.ShapeDtypeStruct((n, d), F32), jax.ShapeDtypeStruct((n, d), BF16)],
        compiler_params=_cparams(("arbitrary",)),
        name="out_proj_norm2",
    )(x, merged, w_out, g2.reshape(1, d))


def _peer_scores_kernel(h_ref, wq_ref, k1_ref, k2_ref, s_ref):
    q = jnp.dot(h_ref[...], wq_ref[...], preferred_element_type=F32).astype(BF16)
    half = q.shape[1] // (2 * PEER_HEADS)
    for h in range(PEER_HEADS):
        for c, kr in enumerate((k1_ref, k2_ref)):
            qs = q[:, (2 * h + c) * half:(2 * h + c + 1) * half]
            s_ref[2 * h + c] = lax.dot_general(kr[h], qs, _NT, preferred_element_type=F32)


def _peer_scores(h2, wq, k1, k2):
    n, d = h2.shape
    tm = _row_tile(n, (256, 128))
    kspec = pl.BlockSpec(k1.shape, lambda i: (0, 0, 0))
    return pl.pallas_call(
        _peer_scores_kernel,
        grid=(n // tm,),
        in_specs=[pl.BlockSpec((tm, d), lambda i: (i, 0)), pl.BlockSpec(wq.shape, lambda i: (0, 0)), kspec, kspec],
        out_specs=pl.BlockSpec((2 * PEER_HEADS, PEER_NKEYS, tm), lambda i: (0, 0, i)),
        out_shape=jax.ShapeDtypeStruct((2 * PEER_HEADS, PEER_NKEYS, n), F32),
        compiler_params=_cparams(("arbitrary",)),
        name="peer_scores",
    )(h2, wq, k1, k2)


def _cand_layout():
    k = PEER_TOPK
    groups = [(0, k)] + [(j, 8) for j in range(1, 8)]
    pos = []
    for j, rows in groups:
        for l in range(rows):
            pos.append(j * k + l if (j + 1) * (l + 1) <= k else 1 << 20)
    pos += [j * k for j in range(8, k)]
    return groups, pos


def _topk_rows(s, k):
    n_rows = s.shape[0]
    row = lax.broadcasted_iota(I32, s.shape, 0).astype(F32)
    vals, idxs = [], []
    for _ in range(k):
        m = jnp.max(s, axis=0, keepdims=True)
        idx = jnp.min(jnp.where(s == m, row, float(n_rows)), axis=0, keepdims=True)
        vals.append(m)
        idxs.append(idx)
        s = jnp.where(row == idx, NEG_INF, s)
    return jnp.concatenate(vals, axis=0), jnp.concatenate(idxs, axis=0)


def _peer_topk_kernel(s_ref, pos_ref, i1_ref, i2_ref, g_ref):
    k = PEER_TOPK
    groups, _ = _cand_layout()
    pos = pos_ref[...]
    big = float(1 << 20)
    e_rows, g_rows = [], []
    for h in range(PEER_HEADS):
        v1, i1 = _topk_rows(s_ref[2 * h], k)
        v2, i2 = _topk_rows(s_ref[2 * h + 1], k)
        cand, cidx = [], []
        for j, rows in groups:
            cand.append(v1[j:j + 1] + v2[0:rows])
            cidx.append(i1[j:j + 1] * PEER_NKEYS + i2[0:rows])
        cand.append(v1[8:k] + v2[0:1])
        cidx.append(i1[8:k] * PEER_NKEYS + i2[0:1])
        cand = jnp.where(pos < big, jnp.concatenate(cand, axis=0), NEG_INF)
        cidx = jnp.concatenate(cidx, axis=0)
        sc, ex = [], []
        for _ in range(k):
            m = jnp.max(cand, axis=0, keepdims=True)
            pidx = jnp.min(jnp.where(cand == m, pos, big), axis=0, keepdims=True)
            sel = pos == pidx
            sc.append(m)
            ex.append(jnp.max(jnp.where(sel, cidx, -1.0), axis=0, keepdims=True))
            cand = jnp.where(sel, NEG_INF, cand)
        sc = jnp.concatenate(sc, axis=0)
        ew = jnp.exp(sc - sc[0:1])
        g_rows.append(ew / jnp.sum(ew, axis=0, keepdims=True))
        e_rows.append(jnp.concatenate(ex, axis=0))
    e_all = jnp.concatenate(e_rows, axis=0)
    g_all = jnp.concatenate(g_rows, axis=0)
    first = jnp.floor(e_all * (1.0 / PEER_NKEYS))
    i1_ref[...] = first.T
    i2_ref[...] = (e_all - first * PEER_NKEYS).T
    g_ref[...] = g_all.T


def _peer_topk(scores_t):
    n = scores_t.shape[2]
    tl = LANE
    _, pos = _cand_layout()
    pos_tab = jnp.broadcast_to(jnp.asarray(pos, F32)[:, None], (len(pos), tl))
    nsel = PEER_HEADS * PEER_TOPK
    ospec = pl.BlockSpec((tl, nsel), lambda i: (i, 0))
    oshape = jax.ShapeDtypeStruct((n, nsel), F32)
    return pl.pallas_call(
        _peer_topk_kernel,
        grid=(n // tl,),
        in_specs=[pl.BlockSpec((2 * PEER_HEADS, PEER_NKEYS, tl), lambda i: (0, 0, i)),
                  pl.BlockSpec(pos_tab.shape, lambda i: (0, 0))],
        out_specs=[ospec, ospec, ospec],
        out_shape=[oshape, oshape, oshape],
        compiler_params=_cparams(("arbitrary",)),
        name="peer_topk",
    )(scores_t, pos_tab)


def _peer_gates_kernel(i1_ref, i2_ref, g_ref, o_ref, x_scr):
    tb = i1_ref.shape[0]
    grp = 16
    sub = lax.broadcasted_iota(I32, (PEER_NKEYS, LANE), 0).astype(F32)
    ro = lax.broadcasted_iota(I32, (grp * grp, grp * grp), 0)
    ci = lax.broadcasted_iota(I32, (grp * grp, grp * grp), 1)
    swap = jnp.where((ro % grp) * grp + ro // grp == ci, 1.0, 0.0).astype(BF16)

    def token_gates(r0, slot):
        for t in range(grp):
            i1 = i1_ref[pl.ds(r0 + t, 1), :]
            i2 = i2_ref[pl.ds(r0 + t, 1), :]
            g = g_ref[pl.ds(r0 + t, 1), :]
            p1 = jnp.where(i1 == sub, g, 0.0).astype(BF16)
            p2 = jnp.where(i2 == sub, 1.0, 0.0).astype(BF16)
            x_scr[slot, t * PEER_NKEYS:(t + 1) * PEER_NKEYS, :] = lax.dot_general(
                p1, p2, _NT, preferred_element_type=F32).astype(BF16)

    def regroup(r0, slot):
        for kb in range(PEER_NKEYS // grp):
            rows = [x_scr[slot, t * PEER_NKEYS + kb * grp:t * PEER_NKEYS + (kb + 1) * grp, :] for t in range(grp)]
            w = jnp.dot(swap, jnp.concatenate(rows, axis=0), preferred_element_type=F32)
            for j in range(grp):
                o_ref[kb * grp + j, pl.ds(r0, grp), :] = w[j * grp:(j + 1) * grp].astype(BF16)

    n_slot = x_scr.shape[0]

    def groups(gi, carry):
        r0 = pl.multiple_of(gi * (n_slot * grp), n_slot * grp)
        for slot in range(n_slot):
            token_gates(r0 + slot * grp, slot)
        for slot in range(n_slot):
            regroup(r0 + slot * grp, slot)
        return carry

    lax.fori_loop(0, tb // (n_slot * grp), groups, 0)


def _peer_gates(i1, i2, g):
    n, nsel = i1.shape
    tb = LANE
    ispec = pl.BlockSpec((tb, nsel), lambda i: (i, 0))
    return pl.pallas_call(
        _peer_gates_kernel,
        grid=(n // tb,),
        in_specs=[ispec, ispec, ispec],
        out_specs=pl.BlockSpec((PEER_NKEYS, tb, PEER_NKEYS), lambda i: (0, i, 0)),
        out_shape=jax.ShapeDtypeStruct((PEER_NKEYS, n, PEER_NKEYS), BF16),
        scratch_shapes=[pltpu.VMEM((4, 16 * PEER_NKEYS, PEER_NKEYS), BF16)],
        compiler_params=_cparams(("arbitrary",)),
        name="peer_gates",
    )(i1, i2, g)


def _peer_dense_kernel(h_ref, x1_ref, u_ref, v_ref, g_ref, o_ref):
    @pl.when(pl.program_id(1) == 0)
    def _():
        o_ref[...] = x1_ref[...]

    a = lax.dot_general(h_ref[...], u_ref[...], _NT, preferred_element_type=F32)
    ws = []
    for c in range(g_ref.shape[0]):
        ac = a[:, c * LANE:(c + 1) * LANE]
        gelu = 0.5 * ac * (1.0 + lax.erf(ac * SQRT_HALF))
        ws.append((gelu * g_ref[c].astype(F32)).astype(BF16))
    w = jnp.concatenate(ws, axis=1)
    o_ref[...] += jnp.dot(w, v_ref[...], preferred_element_type=F32)


def _peer_dense(h2, x1, u, v, gates):
    n, d = h2.shape
    n_exp = u.shape[0]
    tm = _row_tile(n)
    te = 512
    ge = te // PEER_NKEYS
    row = lambda i, e: (i, 0)
    return pl.pallas_call(
        _peer_dense_kernel,
        grid=(n // tm, n_exp // te),
        in_specs=[
            pl.BlockSpec((tm, d), row),
            pl.BlockSpec((tm, d), row),
            pl.BlockSpec((te, d), lambda i, e: (e, 0)),
            pl.BlockSpec((te, d), lambda i, e: (e, 0)),
            pl.BlockSpec((ge, tm, PEER_NKEYS), lambda i, e: (e, i, 0)),
        ],
        out_specs=pl.BlockSpec((tm, d), row),
        out_shape=jax.ShapeDtypeStruct((n, d), F32),
        compiler_params=_cparams(("arbitrary", "arbitrary")),
        name="peer_dense",
    )(h2, x1, u, v, gates)


def _head_rows_kernel(x_ref, o_ref, *, skip):
    n_pos = o_ref.shape[0] // B_HEADS
    for h in range(B_HEADS):
        o_ref[pl.ds(h, n_pos, stride=B_HEADS), :] = x_ref[skip:skip + n_pos, h * LANE:(h + 1) * LANE]


def _head_rows(x, col_off, *, nb, lp, skip):
    w = B_HEADS * LANE
    n_pos = lp - skip
    out = pl.pallas_call(
        functools.partial(_head_rows_kernel, skip=skip),
        grid=(nb,),
        in_specs=[pl.BlockSpec((lp, w), lambda b: (b, col_off // w))],
        out_specs=pl.BlockSpec((n_pos * B_HEADS, LANE), lambda b: (b, 0)),
        out_shape=jax.ShapeDtypeStruct((nb * n_pos * B_HEADS, LANE), x.dtype),
        compiler_params=_cparams(("arbitrary",)),
        name="head_rows",
    )(x)
    return out.reshape(nb, n_pos, B_HEADS, LANE)


def kernel(x_prompt, x_sample, cache_k, cache_v, state_hgrn, page_table, meta_tokens, lb_logits, rel_bias,
           norm1_g, w_in, a_onorm_g, a_proj, b_qnorm_g, b_knorm_g, b_lambda, b_onorm_g, b_proj, w_out,
           norm2_g, peer_wq, peer_k1, peer_k2, peer_u, peer_v):
    assert w_in.shape[0] == 1, "one layer"
    nb, seq, d = x_prompt.shape
    nbs, t = x_sample.shape[:2]
    lp = seq + LANE
    n_p, n_s = nb * lp, nbs * t
    n = n_p + n_s

    front = jnp.concatenate([jnp.zeros((PAD_FRONT, d), F32), meta_tokens.astype(F32)], axis=0)
    pieces = []
    for b in range(nb):
        pieces += [front, x_prompt[b]]
    x = jnp.concatenate(pieces + [x_sample.reshape(n_s, d)], axis=0)

    bf = lambda w: w.astype(BF16)
    z = _rms_matmul(x, norm1_g[0], bf(w_in[0]), 1024)

    a_p, s_p = _hgrn(z, lb_logits, a_onorm_g[0], jnp.zeros((nb, A_HEADS, A_DK, A_DV), F32),
                     row0=0, nb=nb, t_len=lp, c=HGRN_CHUNK, hp=HGRN_HEADS_PER_STEP)
    a_s, s_s = _hgrn(z, lb_logits, a_onorm_g[0], state_hgrn[0], row0=n_p, nb=nbs, t_len=t, c=t, hp=A_HEADS)

    qn, qt, kn, knb, vt = _qk_norm(z, b_qnorm_g[0], b_knorm_g[0])
    b_p = _attn_prompt(qt, knb, vt, rel_bias, b_lambda[0], b_onorm_g[0], nb=nb, lp=lp)
    w = B_HEADS * B_DV
    b_s = _attn_decode(qn[n_p:].astype(F32), kn[n_p:], z[n_p:, OFF_BV:OFF_BV + w], cache_k, cache_v, page_table,
                       rel_bias, b_lambda[0], b_onorm_g[0], nb=nbs, t=t)

    merged = _merge(a_p, a_s, b_p, b_s, z, bf(a_proj[0]), bf(b_proj[0]))
    x1, h2 = _out_proj(x, merged, bf(w_out[0]), norm2_g[0])

    scores_t = _peer_scores(h2, bf(peer_wq[0]), bf(peer_k1[0]), bf(peer_k2[0]))
    i1, i2, g = _peer_topk(scores_t)
    gates = _peer_gates(i1, i2, g)
    y = _peer_dense(h2, x1, bf(peer_u[0]), bf(peer_v[0]), gates)

    y_prompt = y[:n_p].reshape(nb, lp, d)[:, LANE:]
    y_sample = y[n_p:].reshape(nbs, t, d)
    k_prompt = _head_rows(kn, 0, nb=nb, lp=lp, skip=PAD_FRONT)[None]
    v_prompt = _head_rows(z, OFF_BV, nb=nb, lp=lp, skip=PAD_FRONT)[None]
    k_sample = kn[n_p:].reshape(nbs, t, B_HEADS, 2 * B_DH)[None]
    v_sample = z[n_p:, OFF_BV:OFF_BV + w].reshape(nbs, t, B_HEADS, B_DV)[None]
    return (y_prompt, y_sample, k_prompt, v_prompt, s_p[None], k_sample, v_sample, s_s[None])
```

```python
import functools
import math

import jax
import jax.numpy as jnp
from jax import lax
from jax.experimental import pallas as pl
from jax.experimental.pallas import tpu as pltpu

F32 = jnp.float32
BF16 = jnp.bfloat16
I32 = jnp.int32

LANE = 128
D_MODEL = 2048
N_META = 16
PAD_FRONT = LANE - N_META
A_HEADS = 8
A_DK = 128
A_DV = 128
B_HEADS = 8
B_DH = 64
B_DV = 128
N_BUCKETS = 32
MAX_DISTANCE = 128
PEER_HEADS = 8
PEER_NKEYS = 128
PEER_TOPK = 16
EPS = 1e-6
NEG_INF = -1e30
ATTN_SCALE = B_DH ** -0.5
SQRT_HALF = 0.7071067811865476
HGRN_CHUNK = 16
HGRN_HEADS_PER_STEP = 4
VMEM_LIMIT = 56 * 1024 * 1024

OFF_AQ, OFF_AF, OFF_AI, OFF_AG = 0, 1024, 2048, 3072
OFF_BQ, OFF_BK, OFF_BV = 4096, 5120, 6144
OFF_GA, OFF_GB = 7168, 9216
IN_TOTAL = 11264

_NT = (((1,), (1,)), ((), ()))
_TN = (((0,), (0,)), ((), ()))


def _cparams(sem):
    return pltpu.CompilerParams(dimension_semantics=sem, vmem_limit_bytes=VMEM_LIMIT)


def _row_tile(n, prefs=(768, 512, 384, 256, 128)):
    for t in prefs:
        if n % t == 0:
            return t
    raise ValueError(f"token count {n} is not a multiple of 128")


def _rms_matmul_kernel(x_ref, g_ref, w_ref, o_ref, h_ref):
    @pl.when(pl.program_id(1) == 0)
    def _():
        x = x_ref[...]
        ms = jnp.mean(x * x, axis=-1, keepdims=True)
        h_ref[...] = (x * lax.rsqrt(ms + EPS) * g_ref[...]).astype(BF16)

    o_ref[...] = jnp.dot(h_ref[...], w_ref[...], preferred_element_type=F32)


def _rms_matmul(x, g, w_bf16, tn):
    n, d = x.shape
    nc = w_bf16.shape[1]
    tm = _row_tile(n)
    return pl.pallas_call(
        _rms_matmul_kernel,
        grid=(n // tm, nc // tn),
        in_specs=[
            pl.BlockSpec((tm, d), lambda i, j: (i, 0)),
            pl.BlockSpec((1, d), lambda i, j: (0, 0)),
            pl.BlockSpec((d, tn), lambda i, j: (0, j)),
        ],
        out_specs=pl.BlockSpec((tm, tn), lambda i, j: (i, j)),
        out_shape=jax.ShapeDtypeStruct((n, nc), F32),
        scratch_shapes=[pltpu.VMEM((tm, d), BF16)],
        compiler_params=_cparams(("arbitrary", "arbitrary")),
        name="rms_in_proj",
    )(x, g.reshape(1, d), w_bf16)


def _hgrn_kernel(lbl_ref, gn_ref, q_ref, f_ref, i_ref, g_ref, s0_ref, o_ref, sfin_ref, st_scr, *, c, n_chunks, hp):
    for hh in range(hp):
        st_scr[hh] = s0_ref[0, hh].T
    lg = lbl_ref[...]
    e = jnp.exp(lg - jnp.max(lg, axis=0, keepdims=True))
    lb_all = e[0:1] / jnp.sum(e, axis=0, keepdims=True)
    gn = gn_ref[...]
    row = lax.broadcasted_iota(I32, (c, A_DK), 0)
    ones = jnp.ones((A_DK, A_DV), BF16)

    def chunk(ci, carry):
        r0 = pl.multiple_of(ci * c, c)
        for hh in range(hp):
            head_chunk(r0, hh)
        return carry

    def head_chunk(r0, hh):
        sl = slice(hh * A_DK, (hh + 1) * A_DK)
        lb = lb_all[:, sl]
        q = q_ref[pl.ds(r0, c), sl]
        zf = f_ref[pl.ds(r0, c), sl]
        v = i_ref[pl.ds(r0, c), sl]
        zg = g_ref[pl.ds(r0, c), sl]
        f = lb + (1.0 - lb) * jax.nn.sigmoid(zf)
        k = 1.0 - f
        b = jnp.log(f)
        sh = 1
        while sh < c:
            b = b + jnp.where(row >= sh, pltpu.roll(b, sh, 0), 0.0)
            sh *= 2
        sub = 8
        ps = []
        for s in range(c):
            t0 = s // sub * sub
            d = jnp.exp(jnp.where(row[t0:] >= s, b[t0:] - b[s:s + 1, :], NEG_INF))
            ps.append(q[t0:] * d * k[s:s + 1, :])
        p = jnp.concatenate(ps, axis=0).astype(BF16)
        a = jnp.dot(p, ones, preferred_element_type=F32)
        o_groups = [None] * (c // sub)
        off = 0
        for s in range(c):
            t0 = s // sub * sub
            for gi in range(t0 // sub, c // sub):
                r0_ = off + gi * sub - t0
                piece = a[r0_:r0_ + sub] * v[s:s + 1, :]
                o_groups[gi] = piece if o_groups[gi] is None else o_groups[gi] + piece
            off += c - t0
        o = jnp.concatenate(o_groups, axis=0)
        st = st_scr[hh]
        qe = (q * jnp.exp(b)).astype(BF16)
        o = o + lax.dot_general(qe, st.astype(BF16), _NT, preferred_element_type=F32)
        bc = b[c - 1:c, :]
        kt = (k * jnp.exp(bc - b)).astype(BF16)
        upd = lax.dot_general(v.astype(BF16), kt, _TN, preferred_element_type=F32)
        st_scr[hh] = st * jnp.exp(bc) + upd
        ms = jnp.mean(o * o, axis=-1, keepdims=True)
        o_ref[pl.ds(r0, c), sl] = o * lax.rsqrt(ms + EPS) * gn * (zg * jax.nn.sigmoid(zg))

    lax.fori_loop(0, n_chunks, chunk, 0, unroll=4 if n_chunks % 4 == 0 else 1)
    for hh in range(hp):
        sfin_ref[0, hh] = st_scr[hh].T


def _hgrn(z, lb_logits, gn, s0, *, row0, nb, t_len, c, hp):
    assert row0 % t_len == 0 and t_len % c == 0 and A_HEADS % hp == 0
    rb0 = row0 // t_len
    wb = hp * A_DK
    zspec = lambda off: pl.BlockSpec((t_len, wb), lambda b, h, off=off: (rb0 + b, off // wb + h))
    return pl.pallas_call(
        functools.partial(_hgrn_kernel, c=c, n_chunks=t_len // c, hp=hp),
        grid=(nb, A_HEADS // hp),
        in_specs=[
            pl.BlockSpec((lb_logits.shape[0], wb), lambda b, h: (0, h)),
            pl.BlockSpec((1, A_DV), lambda b, h: (0, 0)),
            zspec(OFF_AQ), zspec(OFF_AF), zspec(OFF_AI), zspec(OFF_AG),
            pl.BlockSpec((1, hp, A_DK, A_DV), lambda b, h: (b, h, 0, 0)),
        ],
        out_specs=[
            pl.BlockSpec((t_len, wb), lambda b, h: (b, h)),
            pl.BlockSpec((1, hp, A_DK, A_DV), lambda b, h: (b, h, 0, 0)),
        ],
        out_shape=[
            jax.ShapeDtypeStruct((nb * t_len, A_HEADS * A_DV), F32),
            jax.ShapeDtypeStruct((nb, A_HEADS, A_DK, A_DV), F32),
        ],
        scratch_shapes=[pltpu.VMEM((hp, A_DV, A_DK), F32)],
        compiler_params=_cparams(("arbitrary", "arbitrary")),
        name=f"hgrn_c{c}",
    )(lb_logits, gn.reshape(1, A_DV), z, z, z, z, s0)


def _qk_norm_kernel(zq_ref, zk_ref, zv_ref, qg_ref, kg_ref, qn_ref, qt_ref, kn_ref, knb_ref, vt_ref):
    r = lax.broadcasted_iota(I32, (LANE, LANE), 0) // B_DH
    cc = lax.broadcasted_iota(I32, (LANE, LANE), 1) // B_DH
    seg = jnp.where(r == cc, 1.0, 0.0).astype(BF16)

    def norm(z, g):
        sq = z * z
        hi = sq.astype(BF16)
        lo = (sq - hi.astype(F32)).astype(BF16)
        ss = jnp.dot(hi, seg, preferred_element_type=F32) + jnp.dot(lo, seg, preferred_element_type=F32)
        return z * lax.rsqrt(ss * (1.0 / B_DH) + EPS) * g

    qg = qg_ref[...]
    kg = kg_ref[...]
    for h in range(B_HEADS):
        sl = slice(h * LANE, (h + 1) * LANE)
        qn = norm(zq_ref[:, sl], qg) * ATTN_SCALE
        qn_ref[:, sl] = qn.astype(BF16)
        qt_ref[sl, :] = qn.T.astype(BF16)
        kn = norm(zk_ref[:, sl], kg)
        kn_ref[:, sl] = kn
        knb_ref[:, sl] = kn.astype(BF16)
        vt_ref[sl, :] = zv_ref[:, sl].T.astype(BF16)


def _qk_norm(z, qg, kg):
    n = z.shape[0]
    tm = _row_tile(n)
    w = B_HEADS * 2 * B_DH
    zspec = lambda off: pl.BlockSpec((tm, w), lambda i, off=off: (i, off // w))
    ospec = pl.BlockSpec((tm, w), lambda i: (i, 0))
    tspec = pl.BlockSpec((w, tm), lambda i: (0, i))
    gspec = pl.BlockSpec((1, LANE), lambda i: (0, 0))
    return pl.pallas_call(
        _qk_norm_kernel,
        grid=(n // tm,),
        in_specs=[zspec(OFF_BQ), zspec(OFF_BK), zspec(OFF_BV), gspec, gspec],
        out_specs=[ospec, tspec, ospec, ospec, tspec],
        out_shape=[
            jax.ShapeDtypeStruct((n, w), BF16),
            jax.ShapeDtypeStruct((w, n), BF16),
            jax.ShapeDtypeStruct((n, w), F32),
            jax.ShapeDtypeStruct((n, w), BF16),
            jax.ShapeDtypeStruct((w, n), BF16),
        ],
        compiler_params=_cparams(("arbitrary",)),
        name="qk_norm",
    )(z, z, z, jnp.tile(qg, 2).reshape(1, LANE), jnp.tile(kg, 2).reshape(1, LANE))


def _rel_bucket(dist):
    n = jnp.maximum(dist, 0)
    max_exact = N_BUCKETS // 2
    nf = jnp.maximum(n, 1).astype(F32)
    large = max_exact + (jnp.log(nf / max_exact) / math.log(MAX_DISTANCE / max_exact)
                         * (N_BUCKETS - max_exact)).astype(I32)
    large = jnp.minimum(large, N_BUCKETS - 1)
    return jnp.where(n < max_exact, n, large)


def _bias_from_bucket(bucket, rb_ref, h):
    val = jnp.zeros(bucket.shape, F32)
    for bk in range(N_BUCKETS):
        val = jnp.where(bucket == bk, rb_ref[bk, h], val)
    return val


def _diff_lambda(lam_ref):
    lp = lam_ref[...]
    lam_init = 0.8 - 0.6 * math.exp(-0.3 * 0)
    lam = (jnp.exp(jnp.sum(lp[0:1] * lp[1:2], axis=-1, keepdims=True))
           - jnp.exp(jnp.sum(lp[2:3] * lp[3:4], axis=-1, keepdims=True)) + lam_init)
    return lam, lam_init


def _split_maps(qh):
    lane = lax.broadcasted_iota(I32, qh.shape, 1)
    zero = jnp.zeros_like(qh)
    return jnp.concatenate([jnp.where(lane < B_DH, qh, zero), jnp.where(lane >= B_DH, qh, zero)], axis=0)


def _online_update(carry, s, vj):
    m, l, acc = carry
    m_new = jnp.maximum(m, jnp.max(s, axis=-1, keepdims=True))
    a = jnp.exp(m - m_new)
    p = jnp.exp(s - m_new)
    l = l * a + jnp.sum(p, axis=-1, keepdims=True)
    acc = acc * a + jnp.dot(p.astype(BF16), vj, preferred_element_type=F32)
    return m_new, l, acc


def _diff_finish(carry, t, lam, lam_init, og):
    _, l, acc = carry
    o = acc[0:t] / l[0:t] - lam * (acc[t:2 * t] / l[t:2 * t])
    ms = jnp.mean(o * o, axis=-1, keepdims=True)
    return o * lax.rsqrt(ms + EPS) * og * (1.0 - lam_init)


_BIAS_DIAG, _BIAS_NEAR, _BIAS_DIAG_PAD, _BIAS_NEAR_PAD, _BIAS_FAR_PAD = range(5)


def _attn_prompt_kernel(rb_ref, lam_ref, og_ref, qt_ref, k_ref, vt_ref, o_ref, bias_scr, q2_scr, m_scr, l_scr, acc_scr):
    i = pl.program_id(1)
    tq = qt_ref.shape[1]

    @pl.when((pl.program_id(0) == 0) & (i == 0))
    def _():
        r = lax.broadcasted_iota(I32, (tq, 2 * tq), 0)
        qi = lax.broadcasted_iota(I32, (tq, 2 * tq), 1) % tq
        in_pad = r < PAD_FRONT
        for blk in range(2):
            d = qi - r + blk * tq
            bucket = _rel_bucket(d)
            for h in range(B_HEADS):
                val = _bias_from_bucket(bucket, rb_ref, h)
                if blk == 0:
                    val = jnp.where(d >= 0, val, NEG_INF)
                bias_scr[_BIAS_DIAG + blk, h] = val
                bias_scr[_BIAS_DIAG_PAD + blk, h] = jnp.where(in_pad, NEG_INF, val)
        for h in range(B_HEADS):
            bias_scr[_BIAS_FAR_PAD, h] = jnp.where(in_pad, NEG_INF, rb_ref[N_BUCKETS - 1, h])

    row = lax.broadcasted_iota(I32, (LANE, tq), 0)
    for h in range(B_HEADS):
        qt = qt_ref[h * LANE:(h + 1) * LANE, :]
        zero = jnp.zeros_like(qt)
        q2_scr[h] = jnp.concatenate([jnp.where(row < B_DH, qt, zero), jnp.where(row >= B_DH, qt, zero)], axis=1)
    m_scr[...] = jnp.full(m_scr.shape, NEG_INF, F32)
    l_scr[...] = jnp.zeros(l_scr.shape, F32)
    acc_scr[...] = jnp.zeros(acc_scr.shape, F32)

    def step(j, bias_of):
        k0 = pl.multiple_of(j * tq, tq)
        for h in range(B_HEADS):
            sl = slice(h * LANE, (h + 1) * LANE)
            s = jnp.dot(k_ref[pl.ds(k0, tq), sl], q2_scr[h], preferred_element_type=F32) + bias_of(h)
            m = m_scr[h]
            m_new = jnp.maximum(m, jnp.max(s, axis=0, keepdims=True))
            a = jnp.exp(m - m_new)
            p = jnp.exp(s - m_new)
            m_scr[h] = m_new
            l_scr[h] = l_scr[h] * a + jnp.sum(p, axis=0, keepdims=True)
            acc_scr[h] = acc_scr[h] * a + jnp.dot(vt_ref[sl, pl.ds(k0, tq)], p.astype(BF16),
                                                  preferred_element_type=F32)

    first = jnp.where(i == 0, _BIAS_DIAG_PAD, jnp.where(i == 1, _BIAS_NEAR_PAD, _BIAS_FAR_PAD))
    step(0, lambda h: bias_scr[first, h])

    def far_body(j, carry):
        step(j, lambda h: rb_ref[N_BUCKETS - 1, h])
        return carry

    lax.fori_loop(1, jnp.maximum(i - 1, 1), far_body, 0)

    @pl.when(i >= 2)
    def _():
        step(i - 1, lambda h: bias_scr[_BIAS_NEAR, h])

    @pl.when(i >= 1)
    def _():
        step(i, lambda h: bias_scr[_BIAS_DIAG, h])

    lam, lam_init = _diff_lambda(lam_ref)
    og = og_ref[...]
    for h in range(B_HEADS):
        acc = acc_scr[h]
        l = l_scr[h]
        o = (acc[:, :tq] / l[:, :tq] - lam * (acc[:, tq:] / l[:, tq:])).T
        ms = jnp.mean(o * o, axis=-1, keepdims=True)
        o_ref[:, h * LANE:(h + 1) * LANE] = o * lax.rsqrt(ms + EPS) * og * (1.0 - lam_init)


def _attn_prompt(qt, knb, vt, rel_bias, b_lambda, og, *, nb, lp):
    tq = LANE
    nq = lp // tq
    w = B_HEADS * B_DV
    return pl.pallas_call(
        _attn_prompt_kernel,
        grid=(nb, nq),
        in_specs=[
            pl.BlockSpec(memory_space=pltpu.SMEM),
            pl.BlockSpec((4, B_DH), lambda b, i: (0, 0)),
            pl.BlockSpec((1, B_DV), lambda b, i: (0, 0)),
            pl.BlockSpec((w, tq), lambda b, i: (0, b * nq + i)),
            pl.BlockSpec((lp, w), lambda b, i: (b, 0)),
            pl.BlockSpec((w, lp), lambda b, i: (0, b)),
        ],
        out_specs=pl.BlockSpec((tq, w), lambda b, i: (b * nq + i, 0)),
        out_shape=jax.ShapeDtypeStruct((nb * lp, w), F32),
        scratch_shapes=[
            pltpu.VMEM((5, B_HEADS, tq, 2 * tq), F32),
            pltpu.VMEM((B_HEADS, LANE, 2 * tq), BF16),
            pltpu.VMEM((B_HEADS, 1, 2 * tq), F32),
            pltpu.VMEM((B_HEADS, 1, 2 * tq), F32),
            pltpu.VMEM((B_HEADS, B_DV, 2 * tq), F32),
        ],
        compiler_params=_cparams(("arbitrary", "arbitrary")),
        name="attn_prompt",
    )(rel_bias, b_lambda, og.reshape(1, B_DV), qt, knb, vt)


def _attn_decode_kernel(pt_ref, rb_ref, lam_ref, og_ref, q_ref, kn_ref, vn_ref, *rest, pages_per_step):
    kp = rest[:pages_per_step]
    vp = rest[pages_per_step:2 * pages_per_step]
    o_ref, bias_scr, qbd_scr, m_scr, l_scr, acc_scr = rest[2 * pages_per_step:]
    del pt_ref
    b = pl.program_id(0)
    p = pl.program_id(1)
    n_steps = pl.num_programs(1)
    t = q_ref.shape[0]
    page = kp[0].shape[1] // B_HEADS
    rows = 2 * t
    n_col = B_HEADS * rows
    pair = 2 * LANE
    n_pair = B_HEADS // 2

    def head_rows(ref, h):
        return ref[0, pl.ds(h, page, stride=B_HEADS), :].astype(BF16)

    @pl.when((b == 0) & (p == 0))
    def _():
        key = lax.broadcasted_iota(I32, (page, n_col), 0)
        col = lax.broadcasted_iota(I32, (page, n_col), 1)
        tok = col % t
        bk_last = _rel_bucket(page - key + tok)
        d_self = tok - key
        bk_self = _rel_bucket(d_self)
        v_last = jnp.zeros((page, n_col), F32)
        v_self = jnp.zeros((page, n_col), F32)
        v_far = jnp.zeros((page, n_col), F32)
        for h in range(B_HEADS):
            mine = col // rows == h
            v_last = jnp.where(mine, _bias_from_bucket(bk_last, rb_ref, h), v_last)
            v_self = jnp.where(mine, _bias_from_bucket(bk_self, rb_ref, h), v_self)
            v_far = jnp.where(mine, rb_ref[N_BUCKETS - 1, h], v_far)
        bias_scr[0] = v_last
        bias_scr[1] = jnp.where((d_self >= 0) & (key < t), v_self, NEG_INF)
        bias_scr[2] = v_far

    @pl.when(p == 0)
    def _():
        m_scr[...] = jnp.full(m_scr.shape, NEG_INF, F32)
        l_scr[...] = jnp.zeros(l_scr.shape, F32)
        acc_scr[...] = jnp.zeros(acc_scr.shape, F32)
        w = B_HEADS * LANE
        qrep = jnp.concatenate([q_ref[...]] * (n_col // t), axis=0)
        r = lax.broadcasted_iota(I32, (n_col, w), 0)
        f = lax.broadcasted_iota(I32, (n_col, w), 1)
        keep = (f // LANE == r // rows) & ((f % LANE) // B_DH == (r // t) % 2)
        qbd = jnp.where(keep, qrep, 0.0)
        for g in range(n_pair):
            for hh in range(2):
                c0 = g * pair + hh * LANE
                qbd_scr[g, hh * LANE:(hh + 1) * LANE, :] = qbd[:, c0:c0 + LANE].T.astype(BF16)

    far_row = bias_scr[2, 0:1, :]

    def key_scores(get_pair):
        s = jnp.dot(get_pair(0), qbd_scr[0], preferred_element_type=F32)
        for g in range(1, n_pair):
            s = s + jnp.dot(get_pair(g), qbd_scr[g], preferred_element_type=F32)
        return s

    def online_update(s, values):
        st = jnp.concatenate(s, axis=0)
        m = m_scr[...]
        m_new = jnp.maximum(m, jnp.max(st, axis=0, keepdims=True))
        a = jnp.exp(m - m_new)
        pt_ = jnp.exp(st - m_new)
        m_scr[...] = m_new
        l_scr[...] = l_scr[...] * a + jnp.sum(pt_, axis=0, keepdims=True)
        a_col = jnp.broadcast_to(a, (n_col, n_col)).T
        pq = [pt_[i * page:(i + 1) * page].T.astype(BF16) for i in range(len(s))]
        for h in range(B_HEADS):
            hs = slice(h * rows, (h + 1) * rows)
            pv = jnp.dot(pq[0][hs], values[0](h), preferred_element_type=F32)
            for i in range(1, len(s)):
                pv = pv + jnp.dot(pq[i][hs], values[i](h), preferred_element_type=F32)
            acc_scr[h] = acc_scr[h] * a_col[hs] + pv

    blocks = []
    for i in range(pages_per_step):
        s = key_scores(lambda g, i=i: jnp.concatenate([head_rows(kp[i], 2 * g), head_rows(kp[i], 2 * g + 1)], axis=1))
        if i == pages_per_step - 1:
            blocks.append(s + jnp.where(p == n_steps - 1, bias_scr[0], far_row))
        else:
            blocks.append(s + far_row)
    online_update(blocks, [lambda h, i=i: head_rows(vp[i], h) for i in range(pages_per_step)])

    @pl.when(p == n_steps - 1)
    def _():
        lam, lam_init = _diff_lambda(lam_ref)
        og = og_ref[...]
        kn = jnp.concatenate([kn_ref[...].astype(BF16), jnp.zeros((page - t, B_HEADS * LANE), BF16)], axis=0)
        s = key_scores(lambda g: kn[:, g * pair:(g + 1) * pair]) + bias_scr[1]
        zpad = jnp.zeros((page - t, LANE), BF16)
        online_update([s], [lambda h: jnp.concatenate([vn_ref[:, h * LANE:(h + 1) * LANE].astype(BF16), zpad], axis=0)])
        l_col = jnp.broadcast_to(l_scr[...], (n_col, n_col)).T
        for h in range(B_HEADS):
            hs = slice(h * rows, (h + 1) * rows)
            o_ref[:, h * LANE:(h + 1) * LANE] = _diff_finish((None, l_col[hs], acc_scr[h]), t, lam, lam_init, og)


def _attn_decode(qn, kn, vn, cache_k, cache_v, page_table, rel_bias, b_lambda, og, *, nb, t):
    n_pages = page_table.shape[1]
    n_phys, page = cache_k.shape[1], cache_k.shape[2]
    w = B_HEADS * B_DV
    n_col = B_HEADS * 2 * t
    assert n_col == LANE and page == LANE, "one score lane per (head, map, new token)"
    pps = 8
    while n_pages % pps:
        pps //= 2
    tok = pl.BlockSpec((t, w), lambda b, p, pt: (b, 0))
    ck = cache_k.reshape(n_phys, page * B_HEADS, B_DV)
    cv = cache_v.reshape(n_phys, page * B_HEADS, B_DV)
    pspec = lambda i: pl.BlockSpec((1, page * B_HEADS, B_DV), lambda b, p, pt, i=i: (pt[b, p * pps + i], 0, 0))
    grid_spec = pltpu.PrefetchScalarGridSpec(
        num_scalar_prefetch=1,
        grid=(nb, n_pages // pps),
        in_specs=[
            pl.BlockSpec(memory_space=pltpu.SMEM),
            pl.BlockSpec((4, B_DH), lambda b, p, pt: (0, 0)),
            pl.BlockSpec((1, B_DV), lambda b, p, pt: (0, 0)),
            tok, tok, tok,
        ] + [pspec(i) for i in range(pps)] + [pspec(i) for i in range(pps)],
        out_specs=pl.BlockSpec((t, w), lambda b, p, pt: (b, 0)),
        scratch_shapes=[
            pltpu.VMEM((3, page, n_col), F32),
            pltpu.VMEM((B_HEADS // 2, 2 * LANE, n_col), BF16),
            pltpu.VMEM((1, n_col), F32),
            pltpu.VMEM((1, n_col), F32),
            pltpu.VMEM((B_HEADS, 2 * t, B_DV), F32),
        ],
    )
    return pl.pallas_call(
        functools.partial(_attn_decode_kernel, pages_per_step=pps),
        grid_spec=grid_spec,
        out_shape=jax.ShapeDtypeStruct((nb * t, w), F32),
        compiler_params=_cparams(("arbitrary", "arbitrary")),
        name="attn_decode",
    )(page_table, rel_bias, b_lambda, og.reshape(1, B_DV), qn, kn, vn, *([ck] * pps), *([cv] * pps))


def _merge_kernel(ap_ref, as_ref, bp_ref, bs_ref, za0_ref, za1_ref, zb0_ref, zb1_ref, wa_ref, wb_ref, o_ref, *,
                  prompt_tiles):
    is_prompt = pl.program_id(0) < prompt_tiles
    a = jnp.where(is_prompt, ap_ref[...], as_ref[...]).astype(BF16)
    b = jnp.where(is_prompt, bp_ref[...], bs_ref[...]).astype(BF16)
    ya = jnp.dot(a, wa_ref[...], preferred_element_type=F32)
    yb = jnp.dot(b, wb_ref[...], preferred_element_type=F32)
    half = za0_ref.shape[1]
    for c, (za_ref, zb_ref) in enumerate(((za0_ref, zb0_ref), (za1_ref, zb1_ref))):
        cs = slice(c * half, (c + 1) * half)
        o_ref[:, cs] = (jax.nn.sigmoid(za_ref[...]) * ya[:, cs] + jax.nn.sigmoid(zb_ref[...]) * yb[:, cs]).astype(BF16)


def _merge(a_p, a_s, b_p, b_s, z, wa, wb):
    n_p, n_s = a_p.shape[0], a_s.shape[0]
    wa_in, d = wa.shape
    half = d // 2
    tm = next(t for t in (256, 128) if n_p % t == 0 and n_s % t == 0)
    pt = n_p // tm
    prow = lambda w: pl.BlockSpec((tm, w), lambda i: (jnp.minimum(i, pt - 1), 0))
    srow = lambda w: pl.BlockSpec((tm, w), lambda i: (jnp.maximum(i - pt, 0), 0))
    zcol = lambda off: pl.BlockSpec((tm, half), lambda i, off=off: (i, off // half))
    return pl.pallas_call(
        functools.partial(_merge_kernel, prompt_tiles=pt),
        grid=((n_p + n_s) // tm,),
        in_specs=[
            prow(wa_in), srow(wa_in), prow(wb.shape[0]), srow(wb.shape[0]),
            zcol(OFF_GA), zcol(OFF_GA + half), zcol(OFF_GB), zcol(OFF_GB + half),
            pl.BlockSpec(wa.shape, lambda i: (0, 0)),
            pl.BlockSpec(wb.shape, lambda i: (0, 0)),
        ],
        out_specs=pl.BlockSpec((tm, d), lambda i: (i, 0)),
        out_shape=jax.ShapeDtypeStruct((n_p + n_s, d), BF16),
        compiler_params=_cparams(("arbitrary",)),
        name="gated_merge",
    )(a_p, a_s, b_p, b_s, z, z, z, z, wa, wb)


def _out_proj_scores_kernel(x_ref, m_ref, w_ref, g_ref, wq_ref, k1_ref, k2_ref, x1_ref, h2_ref, s_ref):
    x1 = x_ref[...] + jnp.dot(m_ref[...], w_ref[...], preferred_element_type=F32)
    x1_ref[...] = x1
    ms = jnp.mean(x1 * x1, axis=-1, keepdims=True)
    h2 = (x1 * lax.rsqrt(ms + EPS) * g_ref[...]).astype(BF16)
    h2_ref[...] = h2
    q = jnp.dot(h2, wq_ref[...], preferred_element_type=F32).astype(BF16)
    half = q.shape[1] // (2 * PEER_HEADS)
    for h in range(PEER_HEADS):
        for c, kr in enumerate((k1_ref, k2_ref)):
            qs = q[:, (2 * h + c) * half:(2 * h + c + 1) * half]
            s_ref[2 * h + c] = lax.dot_general(kr[h], qs, _NT, preferred_element_type=F32)


def _out_proj_scores(x, merged, w_out, g2, wq, k1, k2):
    n, d = x.shape
    tm = _row_tile(n, (256, 128))
    row = pl.BlockSpec((tm, d), lambda i: (i, 0))
    kspec = pl.BlockSpec(k1.shape, lambda i: (0, 0, 0))
    return pl.pallas_call(
        _out_proj_scores_kernel,
        grid=(n // tm,),
        in_specs=[row, row, pl.BlockSpec((d, d), lambda i: (0, 0)), pl.BlockSpec((1, d), lambda i: (0, 0)),
                  pl.BlockSpec(wq.shape, lambda i: (0, 0)), kspec, kspec],
        out_specs=[row, row, pl.BlockSpec((2 * PEER_HEADS, PEER_NKEYS, tm), lambda i: (0, 0, i))],
        out_shape=[jax.ShapeDtypeStruct((n, d), F32), jax.ShapeDtypeStruct((n, d), BF16),
                   jax.ShapeDtypeStruct((2 * PEER_HEADS, PEER_NKEYS, n), F32)],
        compiler_params=_cparams(("arbitrary",)),
        name="out_proj_scores",
    )(x, merged, w_out, g2.reshape(1, d), wq, k1, k2)


def _cand_layout():
    k = PEER_TOPK
    groups = [(0, k)] + [(j, 8) for j in range(1, 8)]
    pos = []
    for j, rows in groups:
        for l in range(rows):
            pos.append(j * k + l if (j + 1) * (l + 1) <= k else 1 << 20)
    pos += [j * k for j in range(8, k)]
    return groups, pos


def _topk_rows(s, k):
    n_rows = s.shape[0]
    row = lax.broadcasted_iota(I32, s.shape, 0).astype(F32)
    vals, idxs = [], []
    for _ in range(k):
        m = jnp.max(s, axis=0, keepdims=True)
        idx = jnp.min(jnp.where(s == m, row, float(n_rows)), axis=0, keepdims=True)
        vals.append(m)
        idxs.append(idx)
        s = jnp.where(row == idx, NEG_INF, s)
    return jnp.concatenate(vals, axis=0), jnp.concatenate(idxs, axis=0)


def _peer_topk_kernel(s_ref, pos_ref, i1_ref, i2_ref, g_ref):
    k = PEER_TOPK
    groups, _ = _cand_layout()
    pos = pos_ref[...]
    big = float(1 << 20)
    e_rows, g_rows = [], []
    for h in range(PEER_HEADS):
        v1, i1 = _topk_rows(s_ref[2 * h], k)
        v2, i2 = _topk_rows(s_ref[2 * h + 1], k)
        cand, cidx = [], []
        for j, rows in groups:
            cand.append(v1[j:j + 1] + v2[0:rows])
            cidx.append(i1[j:j + 1] * PEER_NKEYS + i2[0:rows])
        cand.append(v1[8:k] + v2[0:1])
        cidx.append(i1[8:k] * PEER_NKEYS + i2[0:1])
        cand = jnp.where(pos < big, jnp.concatenate(cand, axis=0), NEG_INF)
        cidx = jnp.concatenate(cidx, axis=0)
        sc, ex = [], []
        for _ in range(k):
            m = jnp.max(cand, axis=0, keepdims=True)
            pidx = jnp.min(jnp.where(cand == m, pos, big), axis=0, keepdims=True)
            sel = pos == pidx
            sc.append(m)
            ex.append(jnp.max(jnp.where(sel, cidx, -1.0), axis=0, keepdims=True))
            cand = jnp.where(sel, NEG_INF, cand)
        sc = jnp.concatenate(sc, axis=0)
        ew = jnp.exp(sc - sc[0:1])
        g_rows.append(ew / jnp.sum(ew, axis=0, keepdims=True))
        e_rows.append(jnp.concatenate(ex, axis=0))
    e_all = jnp.concatenate(e_rows, axis=0)
    g_all = jnp.concatenate(g_rows, axis=0)
    first = jnp.floor(e_all * (1.0 / PEER_NKEYS))
    i1_ref[...] = first.T
    i2_ref[...] = (e_all - first * PEER_NKEYS).T
    g_ref[...] = g_all.T


def _peer_topk(scores_t):
    n = scores_t.shape[2]
    tl = LANE
    _, pos = _cand_layout()
    pos_tab = jnp.broadcast_to(jnp.asarray(pos, F32)[:, None], (len(pos), tl))
    nsel = PEER_HEADS * PEER_TOPK
    ospec = pl.BlockSpec((tl, nsel), lambda i: (i, 0))
    oshape = jax.ShapeDtypeStruct((n, nsel), F32)
    return pl.pallas_call(
        _peer_topk_kernel,
        grid=(n // tl,),
        in_specs=[pl.BlockSpec((2 * PEER_HEADS, PEER_NKEYS, tl), lambda i: (0, 0, i)),
                  pl.BlockSpec(pos_tab.shape, lambda i: (0, 0))],
        out_specs=[ospec, ospec, ospec],
        out_shape=[oshape, oshape, oshape],
        compiler_params=_cparams(("arbitrary",)),
        name="peer_topk",
    )(scores_t, pos_tab)


def _peer_gates_kernel(i1_ref, i2_ref, g_ref, o_ref, x_scr):
    tb = i1_ref.shape[0]
    grp = 16
    sub = lax.broadcasted_iota(I32, (PEER_NKEYS, LANE), 0).astype(F32)
    ro = lax.broadcasted_iota(I32, (grp * grp, grp * grp), 0)
    ci = lax.broadcasted_iota(I32, (grp * grp, grp * grp), 1)
    swap = jnp.where((ro % grp) * grp + ro // grp == ci, 1.0, 0.0).astype(BF16)

    def token_gates(r0, slot):
        for t in range(grp):
            i1 = i1_ref[pl.ds(r0 + t, 1), :]
            i2 = i2_ref[pl.ds(r0 + t, 1), :]
            g = g_ref[pl.ds(r0 + t, 1), :]
            p1 = jnp.where(i1 == sub, g, 0.0).astype(BF16)
            p2 = jnp.where(i2 == sub, 1.0, 0.0).astype(BF16)
            x_scr[slot, t * PEER_NKEYS:(t + 1) * PEER_NKEYS, :] = lax.dot_general(
                p1, p2, _NT, preferred_element_type=F32).astype(BF16)

    def regroup(r0, slot):
        for kb in range(PEER_NKEYS // grp):
            rows = [x_scr[slot, t * PEER_NKEYS + kb * grp:t * PEER_NKEYS + (kb + 1) * grp, :] for t in range(grp)]
            w = jnp.dot(swap, jnp.concatenate(rows, axis=0), preferred_element_type=F32)
            for j in range(grp):
                o_ref[kb * grp + j, pl.ds(r0, grp), :] = w[j * grp:(j + 1) * grp].astype(BF16)

    n_slot = x_scr.shape[0]

    def groups(gi, carry):
        r0 = pl.multiple_of(gi * (n_slot * grp), n_slot * grp)
        for slot in range(n_slot):
            token_gates(r0 + slot * grp, slot)
        for slot in range(n_slot):
            regroup(r0 + slot * grp, slot)
        return carry

    lax.fori_loop(0, tb // (n_slot * grp), groups, 0)


def _peer_gates(i1, i2, g):
    n, nsel = i1.shape
    tb = LANE
    ispec = pl.BlockSpec((tb, nsel), lambda i: (i, 0))
    return pl.pallas_call(
        _peer_gates_kernel,
        grid=(n // tb,),
        in_specs=[ispec, ispec, ispec],
        out_specs=pl.BlockSpec((PEER_NKEYS, tb, PEER_NKEYS), lambda i: (0, i, 0)),
        out_shape=jax.ShapeDtypeStruct((PEER_NKEYS, n, PEER_NKEYS), BF16),
        scratch_shapes=[pltpu.VMEM((4, 16 * PEER_NKEYS, PEER_NKEYS), BF16)],
        compiler_params=_cparams(("arbitrary",)),
        name="peer_gates",
    )(i1, i2, g)


def _peer_dense_kernel(h_ref, x1_ref, u_ref, v_ref, g_ref, o_ref):
    @pl.when(pl.program_id(1) == 0)
    def _():
        o_ref[...] = x1_ref[...]

    a = lax.dot_general(h_ref[...], u_ref[...], _NT, preferred_element_type=F32)
    ws = []
    for c in range(g_ref.shape[0]):
        ac = a[:, c * LANE:(c + 1) * LANE]
        gelu = 0.5 * ac * (1.0 + lax.erf(ac * SQRT_HALF))
        ws.append((gelu * g_ref[c].astype(F32)).astype(BF16))
    w = jnp.concatenate(ws, axis=1)
    o_ref[...] += jnp.dot(w, v_ref[...], preferred_element_type=F32)


def _peer_dense(h2, x1, u, v, gates):
    n, d = h2.shape
    n_exp = u.shape[0]
    tm = _row_tile(n)
    te = 512
    ge = te // PEER_NKEYS
    row = lambda i, e: (i, 0)
    return pl.pallas_call(
        _peer_dense_kernel,
        grid=(n // tm, n_exp // te),
        in_specs=[
            pl.BlockSpec((tm, d), row),
            pl.BlockSpec((tm, d), row),
            pl.BlockSpec((te, d), lambda i, e: (e, 0)),
            pl.BlockSpec((te, d), lambda i, e: (e, 0)),
            pl.BlockSpec((ge, tm, PEER_NKEYS), lambda i, e: (e, i, 0)),
        ],
        out_specs=pl.BlockSpec((tm, d), row),
        out_shape=jax.ShapeDtypeStruct((n, d), F32),
        compiler_params=_cparams(("arbitrary", "arbitrary")),
        name="peer_dense",
    )(h2, x1, u, v, gates)


def _head_rows_kernel(x_ref, o_ref, *, skip):
    n_pos = o_ref.shape[0] // B_HEADS
    for h in range(B_HEADS):
        o_ref[pl.ds(h, n_pos, stride=B_HEADS), :] = x_ref[skip:skip + n_pos, h * LANE:(h + 1) * LANE]


def _head_rows(x, col_off, *, nb, lp, skip):
    w = B_HEADS * LANE
    n_pos = lp - skip
    out = pl.pallas_call(
        functools.partial(_head_rows_kernel, skip=skip),
        grid=(nb,),
        in_specs=[pl.BlockSpec((lp, w), lambda b: (b, col_off // w))],
        out_specs=pl.BlockSpec((n_pos * B_HEADS, LANE), lambda b: (b, 0)),
        out_shape=jax.ShapeDtypeStruct((nb * n_pos * B_HEADS, LANE), x.dtype),
        compiler_params=_cparams(("arbitrary",)),
        name="head_rows",
    )(x)
    return out.reshape(nb, n_pos, B_HEADS, LANE)


def kernel(x_prompt, x_sample, cache_k, cache_v, state_hgrn, page_table, meta_tokens, lb_logits, rel_bias,
           norm1_g, w_in, a_onorm_g, a_proj, b_qnorm_g, b_knorm_g, b_lambda, b_onorm_g, b_proj, w_out,
           norm2_g, peer_wq, peer_k1, peer_k2, peer_u, peer_v):
    assert w_in.shape[0] == 1, "one layer"
    nb, seq, d = x_prompt.shape
    nbs, t = x_sample.shape[:2]
    lp = seq + LANE
    n_p, n_s = nb * lp, nbs * t
    n = n_p + n_s

    front = jnp.concatenate([jnp.zeros((PAD_FRONT, d), F32), meta_tokens.astype(F32)], axis=0)
    pieces = []
    for b in range(nb):
        pieces += [front, x_prompt[b]]
    x = jnp.concatenate(pieces + [x_sample.reshape(n_s, d)], axis=0)

    bf = lambda w: w.astype(BF16)
    z = _rms_matmul(x, norm1_g[0], bf(w_in[0]), 1024)

    a_p, s_p = _hgrn(z, lb_logits, a_onorm_g[0], jnp.zeros((nb, A_HEADS, A_DK, A_DV), F32),
                     row0=0, nb=nb, t_len=lp, c=HGRN_CHUNK, hp=HGRN_HEADS_PER_STEP)
    a_s, s_s = _hgrn(z, lb_logits, a_onorm_g[0], state_hgrn[0], row0=n_p, nb=nbs, t_len=t, c=t, hp=A_HEADS)

    qn, qt, kn, knb, vt = _qk_norm(z, b_qnorm_g[0], b_knorm_g[0])
    b_p = _attn_prompt(qt, knb, vt, rel_bias, b_lambda[0], b_onorm_g[0], nb=nb, lp=lp)
    w = B_HEADS * B_DV
    b_s = _attn_decode(qn[n_p:].astype(F32), kn[n_p:], z[n_p:, OFF_BV:OFF_BV + w], cache_k, cache_v, page_table,
                       rel_bias, b_lambda[0], b_onorm_g[0], nb=nbs, t=t)

    merged = _merge(a_p, a_s, b_p, b_s, z, bf(a_proj[0]), bf(b_proj[0]))
    x1, h2, scores_t = _out_proj_scores(x, merged, bf(w_out[0]), norm2_g[0],
                                        bf(peer_wq[0]), bf(peer_k1[0]), bf(peer_k2[0]))

    i1, i2, g = _peer_topk(scores_t)
    gates = _peer_gates(i1, i2, g)
    y = _peer_dense(h2, x1, bf(peer_u[0]), bf(peer_v[0]), gates)

    y_prompt = y[:n_p].reshape(nb, lp, d)[:, LANE:]
    y_sample = y[n_p:].reshape(nbs, t, d)
    k_prompt = _head_rows(kn, 0, nb=nb, lp=lp, skip=PAD_FRONT)[None]
    v_prompt = _head_rows(z, OFF_BV, nb=nb, lp=lp, skip=PAD_FRONT)[None]
    k_sample = kn[n_p:].reshape(nbs, t, B_HEADS, 2 * B_DH)[None]
    v_sample = z[n_p:, OFF_BV:OFF_BV + w].reshape(nbs, t, B_HEADS, B_DV)[None]
    return (y_prompt, y_sample, k_prompt, v_prompt, s_p[None], k_sample, v_sample, s_s[None])
```
